```python
import jax
import jax.numpy as jnp
from jax import lax
import numpy as np


D_MODEL = 1024
BATCH = 2
SEQ = 8192
DEPTH = 4
DEC_BATCH = 32
DEC_SEQ = 1
PAST_LEN = 8192
PAGE_SIZE = 128

H_A = 8
DH_A = 128
W_A = H_A * DH_A
BLOCK_Q = 128
FORGET_BIAS_MEAN = 2.0
N_POOL = 4
POOL_WINDOWS = (2, 4, 8, 16)
POOL_GW = D_MODEL // N_POOL
W_B = N_POOL * POOL_GW
POOL_STATE = 15
N_MEM = 256
H_M = 4
DH_M = D_MODEL // H_M
W_M = H_M * DH_M
IN_SPLITS = (W_A, W_A, W_A, H_A, W_A, W_B, W_B, W_M, W_M, D_MODEL, D_MODEL, D_MODEL)
N_IN = 4 * W_A + H_A + 2 * W_B + 2 * W_M + 3 * D_MODEL
RMS_EPS = 1e-6

kernel_name = 'fox_pool_mem_hybrid_step'


def rmsnorm(x, g):
    x32 = x.astype(jnp.float32)
    y = x32 * lax.rsqrt(jnp.mean(x32 * x32, axis=-1, keepdims=True) + RMS_EPS)
    return (y * g.astype(jnp.float32)).astype(x.dtype)


def split_in(h, w_in_l):
    z = h @ w_in_l
    idx = [int(i) for i in np.cumsum(IN_SPLITS)[:-1]]
    return jnp.split(z, idx, axis=-1)


def forget_attend(q, k, v, cq, ck, q_pos, k_pos):
    logits = jnp.einsum('nqhd,nkhd->nhqk', q, k).astype(jnp.float32) * (DH_A ** -0.5)
    logits = logits + (jnp.swapaxes(cq, 1, 2)[..., :, None] - jnp.swapaxes(ck, 1, 2)[..., None, :])
    mask = k_pos[None, :] <= q_pos[:, None]
    logits = jnp.where(mask, logits, -jnp.inf)
    p = jax.nn.softmax(logits, axis=-1).astype(v.dtype)
    return jnp.einsum('nhqk,nkhd->nqhd', p, v)


def fox_prompt(q, k, v, logf):
    n, s = q.shape[0], q.shape[1]
    c = jnp.cumsum(logf.astype(jnp.float32), axis=1)
    pos = jnp.arange(s)

    def block(i):
        start = i * BLOCK_Q
        qb = lax.dynamic_slice_in_dim(q, start, BLOCK_Q, axis=1)
        cb = lax.dynamic_slice_in_dim(c, start, BLOCK_Q, axis=1)
        q_pos = start + jnp.arange(BLOCK_Q)
        return forget_attend(qb, k, v, cb, c, q_pos, pos)

    o = lax.map(block, jnp.arange(s // BLOCK_Q))
    return jnp.moveaxis(o, 0, 1).reshape(n, s, W_A)


def fox_sample(q, k_new, v_new, logf_new, k_past, v_past, logf_past):
    n, l = q.shape[0], q.shape[1]
    p_len = k_past.shape[1]
    k = jnp.concatenate([k_past, k_new], axis=1)
    v = jnp.concatenate([v_past, v_new], axis=1)
    c = jnp.cumsum(jnp.concatenate([logf_past.astype(jnp.float32), logf_new.astype(jnp.float32)], axis=1), axis=1)
    k_pos = jnp.arange(p_len + l)
    q_pos = p_len + jnp.arange(l)
    o = forget_attend(q, k, v, c[:, p_len:], c, q_pos, k_pos)
    return o.reshape(n, l, W_A)


def pool_mix(u_ctx, u, pos0, w_pool_l, scale_l):
    n, l = u.shape[0], u.shape[1]
    c_len = u_ctx.shape[1]
    u_ext = jnp.concatenate([u_ctx, u], axis=1).astype(jnp.float32)
    cs = jnp.cumsum(u_ext, axis=1)
    cs = jnp.concatenate([jnp.zeros((n, 1, W_B), jnp.float32), cs], axis=1)
    end = cs[:, c_len + 1:]
    t = pos0 + jnp.arange(l)
    means = []
    for g, w in enumerate(POOL_WINDOWS):
        sl = slice(g * POOL_GW, (g + 1) * POOL_GW)
        start = cs[:, c_len + 1 - w: c_len + 1 - w + l, sl]
        cnt = jnp.minimum(w, t + 1).astype(jnp.float32)[None, :, None]
        means.append((end[..., sl] - start) / cnt)
    mean = jnp.concatenate(means, axis=-1)
    d = (mean - u.astype(jnp.float32)).astype(u.dtype).reshape(n, l, N_POOL, POOL_GW)
    y = jnp.einsum('nlgc,gcd->nlgd', d, w_pool_l).reshape(n, l, W_B)
    return y * scale_l


def mem_kv(mem, g, w):
    n = mem.shape[0]
    kv = rmsnorm(mem, g) @ w
    k, v = jnp.split(kv, 2, axis=-1)
    return k.reshape(n, N_MEM, H_M, DH_M), v.reshape(n, N_MEM, H_M, DH_M)


def mem_attend(q, k, v):
    n, l = q.shape[0], q.shape[1]
    logits = jnp.einsum('nlhd,nmhd->nhlm', q, k).astype(jnp.float32) * (DH_M ** -0.5)
    p = jax.nn.softmax(logits, axis=-1).astype(v.dtype)
    return jnp.einsum('nhlm,nmhd->nlhd', p, v).reshape(n, l, W_M)


def merge_out(o_a, g_a, o_b, g_b, o_m, g_m, z_a, z_b, z_m, w_out_l):
    h = (jax.nn.sigmoid(z_a) * o_a * jax.nn.silu(g_a)
         + jax.nn.sigmoid(z_b) * o_b * jax.nn.silu(g_b)
         + jax.nn.sigmoid(z_m) * o_m * jax.nn.silu(g_m))
    return h @ w_out_l


def setup_inputs(seed: int = 0) -> dict:
    key = jax.random.key(seed)
    ks = jax.random.split(key, 24)
    n_pages = PAST_LEN // PAGE_SIZE
    n_used = DEC_BATCH * n_pages
    n_phys = n_used + max(1, n_used // 4)
    nrm = jax.random.normal
    f32 = jnp.float32
    x_prompt = nrm(ks[0], (BATCH, SEQ, D_MODEL), f32)
    x_sample = nrm(ks[1], (DEC_BATCH, DEC_SEQ, D_MODEL), f32)
    mem_prompt = nrm(ks[2], (BATCH, N_MEM, D_MODEL), f32)
    cache_k_attn = nrm(ks[3], (DEPTH, n_phys, PAGE_SIZE, H_A, DH_A), f32)
    cache_v_attn = nrm(ks[4], (DEPTH, n_phys, PAGE_SIZE, H_A, DH_A), f32)
    cache_logf_attn = jax.nn.log_sigmoid(FORGET_BIAS_MEAN + nrm(ks[5], (DEPTH, n_phys, PAGE_SIZE, H_A), f32))
    cache_k_mem = nrm(ks[6], (DEPTH, DEC_BATCH, N_MEM, H_M, DH_M), f32)
    cache_v_mem = nrm(ks[7], (DEPTH, DEC_BATCH, N_MEM, H_M, DH_M), f32)
    state_pool = nrm(ks[8], (DEPTH, DEC_BATCH, POOL_STATE, W_B), f32)
    page_table = jax.random.permutation(ks[9], n_phys)[:n_used].reshape(DEC_BATCH, n_pages).astype(jnp.int32)
    norm_pre = 1.0 + 0.05 * nrm(ks[10], (DEPTH, D_MODEL), f32)
    w_in = nrm(ks[11], (DEPTH, D_MODEL, N_IN), f32) * D_MODEL ** -0.5
    b_forget = FORGET_BIAS_MEAN + 0.5 * nrm(ks[12], (DEPTH, H_A), f32)
    w_pool = nrm(ks[13], (DEPTH, N_POOL, POOL_GW, POOL_GW), f32) * POOL_GW ** -0.5
    pool_scale = 1.0 + 0.05 * nrm(ks[14], (DEPTH, W_B), f32)
    norm_mem = 1.0 + 0.05 * nrm(ks[15], (DEPTH, D_MODEL), f32)
    w_mem_kv = nrm(ks[16], (DEPTH, D_MODEL, 2 * W_M), f32) * D_MODEL ** -0.5
    w_out = nrm(ks[17], (DEPTH, D_MODEL, D_MODEL), f32) * D_MODEL ** -0.5
    norm_post = 1.0 + 0.05 * nrm(ks[18], (DEPTH, D_MODEL), f32)
    return {'x_prompt': x_prompt, 'x_sample': x_sample, 'mem_prompt': mem_prompt,
            'cache_k_attn': cache_k_attn, 'cache_v_attn': cache_v_attn, 'cache_logf_attn': cache_logf_attn,
            'cache_k_mem': cache_k_mem, 'cache_v_mem': cache_v_mem, 'state_pool': state_pool,
            'page_table': page_table, 'norm_pre': norm_pre, 'w_in': w_in, 'b_forget': b_forget,
            'w_pool': w_pool, 'pool_scale': pool_scale, 'norm_mem': norm_mem, 'w_mem_kv': w_mem_kv,
            'w_out': w_out, 'norm_post': norm_post}


def reference(x_prompt, x_sample, mem_prompt, cache_k_attn, cache_v_attn, cache_logf_attn,
              cache_k_mem, cache_v_mem, state_pool, page_table, norm_pre, w_in, b_forget,
              w_pool, pool_scale, norm_mem, w_mem_kv, w_out, norm_post):
    bp, sp = x_prompt.shape[0], x_prompt.shape[1]
    bd, sd = x_sample.shape[0], x_sample.shape[1]
    p_len = page_table.shape[1] * PAGE_SIZE
    xp, xs = x_prompt, x_sample
    kp_l, vp_l, fp_l, kmp_l, vmp_l, pp_l = [], [], [], [], [], []
    kd_l, vd_l, fd_l, pd_l = [], [], [], []
    for l in range(DEPTH):
        h = rmsnorm(xp, norm_pre[l])
        q, k, v, f, g_a, u, g_b, q_m, g_m, z_a, z_b, z_m = split_in(h, w_in[l])
        q = q.reshape(bp, sp, H_A, DH_A)
        k = k.reshape(bp, sp, H_A, DH_A)
        v = v.reshape(bp, sp, H_A, DH_A)
        logf = jax.nn.log_sigmoid((f + b_forget[l]).astype(jnp.float32))
        o_a = fox_prompt(q, k, v, logf)
        o_b = pool_mix(jnp.zeros((bp, POOL_STATE, W_B), u.dtype), u, 0, w_pool[l], pool_scale[l])
        k_m, v_m = mem_kv(mem_prompt, norm_mem[l], w_mem_kv[l])
        o_m = mem_attend(q_m.reshape(bp, sp, H_M, DH_M), k_m, v_m)
        y = merge_out(o_a, g_a, o_b, g_b, o_m, g_m, z_a, z_b, z_m, w_out[l])
        xp = xp + rmsnorm(y, norm_post[l])
        kp_l.append(k)
        vp_l.append(v)
        fp_l.append(logf)
        kmp_l.append(k_m)
        vmp_l.append(v_m)
        pp_l.append(u[:, sp - POOL_STATE:])
        h = rmsnorm(xs, norm_pre[l])
        q, k, v, f, g_a, u, g_b, q_m, g_m, z_a, z_b, z_m = split_in(h, w_in[l])
        q = q.reshape(bd, sd, H_A, DH_A)
        k = k.reshape(bd, sd, H_A, DH_A)
        v = v.reshape(bd, sd, H_A, DH_A)
        logf = jax.nn.log_sigmoid((f + b_forget[l]).astype(jnp.float32))
        k_past = cache_k_attn[l][page_table].reshape(bd, p_len, H_A, DH_A)
        v_past = cache_v_attn[l][page_table].reshape(bd, p_len, H_A, DH_A)
        f_past = cache_logf_attn[l][page_table].reshape(bd, p_len, H_A)
        o_a = fox_sample(q, k, v, logf, k_past, v_past, f_past)
        o_b = pool_mix(state_pool[l], u, p_len, w_pool[l], pool_scale[l])
        o_m = mem_attend(q_m.reshape(bd, sd, H_M, DH_M), cache_k_mem[l], cache_v_mem[l])
        y = merge_out(o_a, g_a, o_b, g_b, o_m, g_m, z_a, z_b, z_m, w_out[l])
        xs = xs + rmsnorm(y, norm_post[l])
        kd_l.append(k)
        vd_l.append(v)
        fd_l.append(logf)
        u_ext = jnp.concatenate([state_pool[l], u], axis=1)
        pd_l.append(u_ext[:, u_ext.shape[1] - POOL_STATE:])
    return (xp, xs,
            jnp.stack(kp_l), jnp.stack(vp_l), jnp.stack(fp_l),
            jnp.stack(kmp_l), jnp.stack(vmp_l), jnp.stack(pp_l),
            jnp.stack(kd_l), jnp.stack(vd_l), jnp.stack(fd_l), jnp.stack(pd_l))
```

```python
import functools

import jax
import jax.numpy as jnp
import numpy as np
from jax import lax
from jax.experimental import pallas as pl
from jax.experimental.pallas import tpu as pltpu

F32 = jnp.float32
BF16 = jnp.bfloat16

RMS_EPS = 1e-6
H_A = 8
DH_A = 128
N_POOL = 4
POOL_WINDOWS = (2, 4, 8, 16)
POOL_STATE = 15
H_M = 4
NEG = -1e30

LANES = 128
SUBLANES = 8
VMEM_BYTES_V7X = 64 * 1024 * 1024

G_K, G_V, G_U, G_Q, G_GA, G_GB, G_QM, G_GM, G_ZA, G_ZB, G_ZM = range(11)
N_F32_GROUPS = 3


def _cparams(sem, vmem_mb):
    return pltpu.CompilerParams(dimension_semantics=sem, vmem_limit_bytes=vmem_mb * 1024 * 1024)


def _rms(x):
    return x * lax.rsqrt(jnp.mean(x * x, axis=-1, keepdims=True) + RMS_EPS)


def _sigmoid(x):
    return 1.0 / (1.0 + jnp.exp(-x))


def _log_sigmoid(x):
    return jnp.minimum(x, 0.0) - jnp.log1p(jnp.exp(-jnp.abs(x)))


def _split3(x):
    hi = x.astype(BF16)
    r1 = x - hi.astype(F32)
    mid = r1.astype(BF16)
    lo = (r1 - mid.astype(F32)).astype(BF16)
    return hi, mid, lo


def _dot(a, b):
    return jnp.dot(a, b, preferred_element_type=F32)


def _dot_nt(a, b):
    return lax.dot_general(a, b, (((1,), (1,)), ((), ())), preferred_element_type=F32)


def _norm_kernel(x_ref, g_ref, o_ref):
    o_ref[...] = (_rms(x_ref[...]) * g_ref[...]).astype(o_ref.dtype)


def _rmsnorm_bf16(x2d, gain, tm):
    m, d = x2d.shape
    return pl.pallas_call(
        _norm_kernel,
        grid=(m // tm,),
        in_specs=[pl.BlockSpec((tm, d), lambda i: (i, 0)),
                  pl.BlockSpec((1, d), lambda i: (0, 0))],
        out_specs=pl.BlockSpec((tm, d), lambda i: (i, 0)),
        out_shape=jax.ShapeDtypeStruct((m, d), BF16),
        compiler_params=_cparams(("arbitrary",), 32),
        name="rmsnorm",
    )(x2d, gain.reshape(1, d))


def _proj_kernel(h_ref, w_ref, *o_refs):
    acc = _dot(h_ref[...], w_ref[...])
    for o in o_refs:
        o[...] = acc.astype(o.dtype)


def _project(h2d, w_groups, g0, ng, out_dtypes, tm):
    m, d = h2d.shape
    n = w_groups.shape[-1]
    return pl.pallas_call(
        _proj_kernel,
        grid=(ng, m // tm),
        in_specs=[pl.BlockSpec((tm, d), lambda g, i: (i, 0)),
                  pl.BlockSpec((None, d, n), lambda g, i: (g + g0, 0, 0))],
        out_specs=[pl.BlockSpec((None, tm, n), lambda g, i: (g, i, 0)) for _ in out_dtypes],
        out_shape=[jax.ShapeDtypeStruct((ng, m, n), dt) for dt in out_dtypes],
        compiler_params=_cparams(("arbitrary", "arbitrary"), 48),
        name="in_proj",
    )(h2d, w_groups)


def _fgate_kernel(h_ref, wf_ref, bf_ref, tri_ref, logf_ref, c_ref, carry_ref):
    @pl.when(pl.program_id(1) == 0)
    def _():
        carry_ref[...] = jnp.zeros_like(carry_ref)

    lf = _log_sigmoid(_dot(h_ref[...], wf_ref[...]) + bf_ref[...])
    logf_ref[...] = lf
    tri = tri_ref[...]
    hi, mid, lo = _split3(lf)
    c = (_dot(tri, hi) + _dot(tri, mid)) + _dot(tri, lo) + carry_ref[0:1, :]
    c_ref[...] = c
    tm = c.shape[0]
    carry_ref[...] = jnp.broadcast_to(c[tm - 1:tm, :], carry_ref.shape)


def _forget_gate(h2d, wf, bf, batch, seq, tm, with_cumsum=True):
    d = h2d.shape[1]
    ns = seq // tm
    tri = jnp.asarray(np.tril(np.ones((tm, tm), np.float32)), BF16)
    return pl.pallas_call(
        _fgate_kernel,
        grid=(batch, ns),
        in_specs=[pl.BlockSpec((tm, d), lambda b, i: (b * ns + i, 0)),
                  pl.BlockSpec((d, LANES), lambda b, i: (0, 0)),
                  pl.BlockSpec((1, LANES), lambda b, i: (0, 0)),
                  pl.BlockSpec((tm, tm), lambda b, i: (0, 0))],
        out_specs=[pl.BlockSpec((tm, LANES), lambda b, i: (b * ns + i, 0)),
                   pl.BlockSpec((tm, LANES), lambda b, i: (b * ns + i, 0))],
        out_shape=[jax.ShapeDtypeStruct((batch * seq, LANES), F32)] * 2,
        scratch_shapes=[pltpu.VMEM((SUBLANES, LANES), F32)],
        compiler_params=_cparams(("arbitrary", "arbitrary"), 32),
        name="forget_gate",
    )(h2d, wf, bf, tri)


def _flash_kernel(q_ref, k_ref, v_ref, cq_ref, ck_ref, o_ref, *, tq, tk, scale):
    qi = pl.program_id(2)
    q = q_ref[...]
    cq = cq_ref[...]
    sub = tq // tk

    def step(kj, carry, masked):
        m, l, acc = carry
        ks = pl.multiple_of(kj * tk, tk)
        k = k_ref[pl.ds(ks, tk), :]
        v = v_ref[pl.ds(ks, tk), :]
        s = _dot_nt(q, k) * scale + (cq - ck_ref[kj])
        if masked:
            row = qi * tq + lax.broadcasted_iota(jnp.int32, (tq, tk), 0)
            col = ks + lax.broadcasted_iota(jnp.int32, (tq, tk), 1)
            s = jnp.where(col <= row, s, NEG)
        m_new = jnp.maximum(m, jnp.max(s, axis=-1, keepdims=True))
        p = jnp.exp(s - m_new)
        alpha = jnp.exp(m - m_new)
        l = alpha * l + jnp.sum(p, axis=-1, keepdims=True)
        acc = alpha * acc + _dot(p.astype(BF16), v)
        return m_new, l, acc

    init = (jnp.full((tq, 1), NEG, F32), jnp.zeros((tq, 1), F32), jnp.zeros((tq, DH_A), F32))
    carry = lax.fori_loop(0, qi * sub, lambda j, c: step(j, c, False), init)
    for d in range(sub):
        carry = step(qi * sub + d, carry, True)
    _, l, acc = carry
    o_ref[...] = (acc / l).astype(o_ref.dtype)


def _fox_prompt(q4, kv4, c_col, c_row, tq, tk):
    _, batch, seq, d = q4.shape
    nk = seq // tk
    kern = functools.partial(_flash_kernel, tq=tq, tk=tk, scale=DH_A ** -0.5)
    return pl.pallas_call(
        kern,
        grid=(batch, H_A, seq // tq),
        in_specs=[pl.BlockSpec((None, None, tq, DH_A), lambda b, h, i: (G_Q - N_F32_GROUPS, b, i, h)),
                  pl.BlockSpec((None, None, seq, DH_A), lambda b, h, i: (G_K, b, 0, h)),
                  pl.BlockSpec((None, None, seq, DH_A), lambda b, h, i: (G_V, b, 0, h)),
                  pl.BlockSpec((None, None, tq, 1), lambda b, h, i: (b, h, i, 0)),
                  pl.BlockSpec((None, None, nk, 1, tk), lambda b, h, i: (b, h, 0, 0, 0))],
        out_specs=pl.BlockSpec((None, tq, DH_A), lambda b, h, i: (b, i, h)),
        out_shape=jax.ShapeDtypeStruct((batch, seq, d), BF16),
        compiler_params=_cparams(("arbitrary", "arbitrary", "arbitrary"), 48),
        name="fox_prompt",
    )(q4, kv4, kv4, c_col, c_row)


def _pool_means_minus_u(ext_ref, s_ref, u, inv_cnt, tm):
    gw = u.shape[1] // N_POOL
    n = tm + 15
    s_ref[0:n, :] = ext_ref[1:1 + n, :] + ext_ref[0:n, :]
    outs = [s_ref[15:15 + tm, 0:gw]]
    n = tm + 13
    s_ref[0:n, gw:] = s_ref[2:2 + n, gw:] + s_ref[0:n, gw:]
    outs.append(s_ref[13:13 + tm, gw:2 * gw])
    n = tm + 9
    s_ref[0:n, 2 * gw:] = s_ref[4:4 + n, 2 * gw:] + s_ref[0:n, 2 * gw:]
    outs.append(s_ref[9:9 + tm, 2 * gw:3 * gw])
    n = tm + 1
    s_ref[0:n, 3 * gw:] = s_ref[8:8 + n, 3 * gw:] + s_ref[0:n, 3 * gw:]
    outs.append(s_ref[1:1 + tm, 3 * gw:])
    return [outs[g] * inv_cnt[g] - u[:, g * gw:(g + 1) * gw] for g in range(N_POOL)]


def _merge(oa, ga, ob, gb, om, gm, za, zb, zm):
    return (_sigmoid(za) * oa * (ga * _sigmoid(ga))
            + _sigmoid(zb) * ob * (gb * _sigmoid(gb))
            + _sigmoid(zm) * om * (gm * _sigmoid(gm)))


def _pmerge_kernel(oa_ref, ga_ref, gb_ref, qm_ref, gm_ref, za_ref, zb_ref, zm_ref, u_ref, x_ref,
                   km_ref, vm_ref, wpool_ref, pscale_ref, wout_ref, npost_ref, npre_ref,
                   xo_ref, ho_ref, ext_ref, s_ref, *, tm):
    i = pl.program_id(1)

    @pl.when(i == 0)
    def _():
        ext_ref[0:16, :] = jnp.zeros((16, ext_ref.shape[1]), F32)

    u = u_ref[...]
    d = u.shape[1]
    ext_ref[16:16 + tm, :] = u
    pos = i * tm + lax.broadcasted_iota(jnp.int32, (tm, 1), 0)
    inv_cnt = [1.0 / jnp.minimum(w, pos + 1).astype(F32) for w in POOL_WINDOWS]
    dm = _pool_means_minus_u(ext_ref, s_ref, u, inv_cnt, tm)
    ob = jnp.concatenate([_dot(dm[g].astype(BF16), wpool_ref[g]) for g in range(N_POOL)], axis=1)
    ob = ob * pscale_ref[...]
    ext_ref[0:16, :] = ext_ref[tm:tm + 16, :]

    dhm = d // H_M
    oms = []
    for hm in range(H_M):
        sl = slice(hm * dhm, (hm + 1) * dhm)
        s = _dot_nt(qm_ref[:, sl], km_ref[:, sl]) * (dhm ** -0.5)
        p = jnp.exp(s - jnp.max(s, axis=-1, keepdims=True))
        den = jnp.sum(p, axis=-1, keepdims=True)
        oms.append(_dot(p.astype(BF16), vm_ref[:, sl]) / den)
    om = jnp.concatenate(oms, axis=1)

    f = lambda r: r[...].astype(F32)
    hmix = _merge(f(oa_ref), f(ga_ref), ob, f(gb_ref), om, f(gm_ref), f(za_ref), f(zb_ref), f(zm_ref))
    y = _dot(hmix.astype(BF16), wout_ref[...])
    xn = x_ref[...] + _rms(y) * npost_ref[...]
    xo_ref[...] = xn
    ho_ref[...] = (_rms(xn) * npre_ref[...]).astype(ho_ref.dtype)


def _prompt_merge(oa, zb8, kvu, x3, kvm, wpool, pscale, wout, npost, npre_next, tm):
    batch, seq, d = x3.shape
    nmem = kvm.shape[2]

    def grp(g):
        return pl.BlockSpec((None, None, tm, d), lambda b, i: (g - N_F32_GROUPS, b, i, 0))

    row = pl.BlockSpec((None, tm, d), lambda b, i: (b, i, 0))
    vec = pl.BlockSpec((1, d), lambda b, i: (0, 0))
    kern = functools.partial(_pmerge_kernel, tm=tm)
    return pl.pallas_call(
        kern,
        grid=(batch, seq // tm),
        in_specs=[row, grp(G_GA), grp(G_GB), grp(G_QM), grp(G_GM), grp(G_ZA), grp(G_ZB), grp(G_ZM),
                  pl.BlockSpec((None, None, tm, d), lambda b, i: (G_U, b, i, 0)),
                  row,
                  pl.BlockSpec((None, None, nmem, d), lambda b, i: (0, b, 0, 0)),
                  pl.BlockSpec((None, None, nmem, d), lambda b, i: (1, b, 0, 0)),
                  pl.BlockSpec(wpool.shape, lambda b, i: (0, 0, 0)),
                  vec,
                  pl.BlockSpec((d, d), lambda b, i: (0, 0)),
                  vec, vec],
        out_specs=[row, row],
        out_shape=[jax.ShapeDtypeStruct((batch, seq, d), F32),
                   jax.ShapeDtypeStruct((batch, seq, d), BF16)],
        scratch_shapes=[pltpu.VMEM((tm + 16, d), F32), pltpu.VMEM((tm + 16, d), F32)],
        compiler_params=_cparams(("arbitrary", "arbitrary"), 56),
        name="prompt_merge",
    )(oa, zb8, zb8, zb8, zb8, zb8, zb8, zb8, kvu, x3, kvm, kvm, wpool, pscale, wout, npost, npre_next)


def _dec_fgate_kernel(h_ref, wf_ref, bf_ref, o_ref):
    o_ref[...] = _log_sigmoid(_dot(h_ref[...], wf_ref[...]) + bf_ref[...])


def _dec_forget_gate(h2d, wf, bf):
    m = h2d.shape[0]
    return pl.pallas_call(
        _dec_fgate_kernel,
        out_shape=jax.ShapeDtypeStruct((m, LANES), F32),
        name="dec_forget_gate",
    )(h2d, wf, bf)


def _dec_bias_kernel(pt_ref, lfnew_ref, sufm_ref, pgm_ref, cache_ref, o_ref, buf_ref, sem_ref, *, layer, npages):
    b = pl.program_id(0)

    def cp(j):
        return pltpu.make_async_copy(cache_ref.at[layer, pt_ref[b, j]], buf_ref.at[j], sem_ref.at[0])

    for j in range(npages):
        cp(j).start()
    for j in range(npages):
        cp(j).wait()

    x = buf_ref[...].reshape(npages * H_A, LANES)
    hi, mid, lo = _split3(x)
    sufm = sufm_ref[...]
    within = (_dot(hi, sufm) + _dot(mid, sufm)) + _dot(lo, sufm)
    ones = jnp.ones((LANES, LANES), BF16)
    tot = (_dot(hi, ones) + _dot(mid, ones)) + _dot(lo, ones)
    th, tmid, tl = _split3(tot)
    pgm = pgm_ref[...]
    later = (_dot(pgm, th) + _dot(pgm, tmid)) + _dot(pgm, tl)
    lfnew = lfnew_ref[...]
    bias = (within + later).reshape(npages, H_A, LANES) + lfnew[None]
    bias = bias.reshape(npages * H_A, LANES)

    bh = bias.astype(BF16).astype(F32)
    bl = (bias - bh).astype(BF16).astype(F32)
    half = LANES // 2
    lane = lax.broadcasted_iota(jnp.int32, bias.shape, 1)
    first = jnp.where(lane < half, bh, pltpu.roll(bl, half, 1))
    second = jnp.where(lane < half, pltpu.roll(bh, half, 1), bl)
    first = first.reshape(npages, 1, H_A, LANES)
    second = second.reshape(npages, 1, H_A, LANES)
    halves = jnp.concatenate([first, second], axis=1)
    o_ref[...] = jnp.concatenate([halves, halves], axis=2).astype(o_ref.dtype)


def _dec_bias(page_table, lfnew_rep, logf_t, layer):
    batch, npages = page_table.shape
    page = logf_t.shape[-1]
    assert page == LANES
    sufm = jnp.asarray(np.tril(np.ones((page, page), np.float32), -1), BF16)
    pg = np.arange(npages * H_A)
    pgm_np = ((pg[None, :] // H_A > pg[:, None] // H_A) & (pg[None, :] % H_A == pg[:, None] % H_A))
    pgm = jnp.asarray(pgm_np.astype(np.float32), BF16)
    kern = functools.partial(_dec_bias_kernel, layer=layer, npages=npages)
    grid_spec = pltpu.PrefetchScalarGridSpec(
        num_scalar_prefetch=1,
        grid=(batch,),
        in_specs=[pl.BlockSpec((None, H_A, LANES), lambda b, pt: (b, 0, 0)),
                  pl.BlockSpec((page, page), lambda b, pt: (0, 0)),
                  pl.BlockSpec(pgm.shape, lambda b, pt: (0, 0)),
                  pl.BlockSpec(memory_space=pl.ANY)],
        out_specs=pl.BlockSpec((None, npages, 2, 2 * H_A, LANES), lambda b, pt: (b, 0, 0, 0, 0)),
        scratch_shapes=[pltpu.VMEM((npages, H_A, LANES), F32), pltpu.SemaphoreType.DMA((1,))],
    )
    return pl.pallas_call(
        kern,
        grid_spec=grid_spec,
        out_shape=jax.ShapeDtypeStruct((batch, npages, 2, 2 * H_A, LANES), BF16),
        compiler_params=_cparams(("arbitrary",), 32),
        name="dec_bias",
    )(page_table, lfnew_rep, sufm, pgm, logf_t)


def _dec_attn_kernel(pt_ref, q_ref, kn_ref, vn_ref, bias_ref, mask_ref, *rest, ppb, scale):
    k_refs = rest[:ppb]
    v_refs = rest[ppb:2 * ppb]
    o_ref = rest[2 * ppb]
    m_ref, l_ref, acc_ref = rest[2 * ppb + 1:]
    j = pl.program_id(1)

    @pl.when(j == 0)
    def _():
        m_ref[...] = jnp.full(m_ref.shape, NEG, F32)
        l_ref[...] = jnp.zeros_like(l_ref)
        acc_ref[...] = jnp.zeros_like(acc_ref)

    qs = q_ref[...] * scale
    ones = jnp.ones((2 * LANES, LANES), BF16)
    page = k_refs[0].shape[0]
    half = page // 2
    m, l, acc = m_ref[...], l_ref[...], acc_ref[...]
    for r in range(ppb):
        k = k_refs[r][...]
        prod = (k * qs[None]).reshape(page * H_A, DH_A).astype(BF16)
        lbs = []
        for hh in range(2):
            t2 = bias_ref[r, hh]
            lbs.append((t2[None] * mask_ref[...]).reshape(half * H_A, LANES))
        lhs = jnp.concatenate([prod, jnp.concatenate(lbs, axis=0)], axis=1)
        s = _dot(lhs, ones).reshape(page, H_A, LANES)
        m_new = jnp.maximum(m, jnp.max(s, axis=0))
        p = jnp.exp(s - m_new[None])
        alpha = jnp.exp(m - m_new)
        l = alpha * l + jnp.sum(p, axis=0)
        acc = alpha * acc + jnp.sum(p * v_refs[r][...], axis=0)
        m = m_new
    m_ref[...], l_ref[...], acc_ref[...] = m, l, acc

    @pl.when(j == pl.num_programs(1) - 1)
    def _():
        s_new = jnp.sum(qs * kn_ref[...], axis=-1, keepdims=True)
        m_fin = jnp.maximum(m, s_new)
        p_new = jnp.exp(s_new - m_fin)
        a = jnp.exp(m - m_fin)
        o_ref[...] = (a * acc + p_new * vn_ref[...]) / (a * l + p_new)


def _dec_attn(page_table, q3, kn3, vn3, bias5, cache_k, cache_v, layer, ppb):
    batch, npages = page_table.shape
    page = cache_k.shape[2]
    half = page // 2
    mask_np = np.zeros((half // 2, 2 * H_A, LANES), np.float32)
    for i in range(half // 2):
        for s in range(2):
            pos = 2 * i + s
            mask_np[i, s * H_A:(s + 1) * H_A, pos] = 1.0
            mask_np[i, s * H_A:(s + 1) * H_A, half + pos] = 1.0
    mask = jnp.asarray(mask_np, BF16)
    vec = pl.BlockSpec((None, H_A, DH_A), lambda b, j, pt: (b, 0, 0))

    def page_spec(r):
        return pl.BlockSpec((None, None, page, H_A, DH_A),
                            lambda b, j, pt: (layer, pt[b, j * ppb + r], 0, 0, 0))

    kern = functools.partial(_dec_attn_kernel, ppb=ppb, scale=DH_A ** -0.5)
    grid_spec = pltpu.PrefetchScalarGridSpec(
        num_scalar_prefetch=1,
        grid=(batch, npages // ppb),
        in_specs=[vec, vec, vec,
                  pl.BlockSpec((None, ppb, 2, 2 * H_A, LANES), lambda b, j, pt: (b, j, 0, 0, 0)),
                  pl.BlockSpec(mask.shape, lambda b, j, pt: (0, 0, 0))]
                 + [page_spec(r) for r in range(ppb)] * 2,
        out_specs=vec,
        scratch_shapes=[pltpu.VMEM((H_A, DH_A), F32)] * 3,
    )
    return pl.pallas_call(
        kern,
        grid_spec=grid_spec,
        out_shape=jax.ShapeDtypeStruct((batch, H_A, DH_A), F32),
        compiler_params=_cparams(("arbitrary", "arbitrary"), 48),
        name="dec_attn",
    )(page_table, q3, kn3, vn3, bias5, mask, *([cache_k] * ppb), *([cache_v] * ppb))


def _dec_mem_kernel(q_ref, k_ref, v_ref, sel_ref, selt_ref, o_ref, *, scale):
    q = q_ref[...] * scale
    prod = (k_ref[...] * q).astype(BF16)
    s = _dot(prod, sel_ref[...])
    p = jnp.exp(s - jnp.max(s, axis=0, keepdims=True))
    pe =_dot(p.astype(BF16), selt_ref[...])
    num = jnp.sum(pe * v_ref[...], axis=0, keepdims=True)
    o_ref[...] = num / jnp.sum(pe, axis=0, keepdims=True)


def _dec_mem_attn(qm, cache_km, cache_vm, layer):
    batch, _, d = qm.shape
    nmem = cache_km.shape[2]
    dhm = d // H_M
    rep = LANES // H_M
    sel_np = (np.arange(d)[:, None] // dhm == np.arange(LANES)[None, :] // rep).astype(np.float32)
    pick_np = (np.arange(LANES)[:, None] == (np.arange(d)[None, :] // dhm) * rep).astype(np.float32)
    kern = functools.partial(_dec_mem_kernel, scale=dhm ** -0.5)
    return pl.pallas_call(
        kern,
        grid=(batch,),
        in_specs=[pl.BlockSpec((None, 1, d), lambda b: (b, 0, 0)),
                  pl.BlockSpec((None, None, nmem, d), lambda b: (layer, b, 0, 0)),
                  pl.BlockSpec((None, None, nmem, d), lambda b: (layer, b, 0, 0)),
                  pl.BlockSpec((d, LANES), lambda b: (0, 0)),
                  pl.BlockSpec((LANES, d), lambda b: (0, 0))],
        out_specs=pl.BlockSpec((None, 1, d), lambda b: (b, 0, 0)),
        out_shape=jax.ShapeDtypeStruct((batch, 1, d), F32),
        compiler_params=_cparams(("arbitrary",), 32),
        name="dec_mem_attn",
    )(qm, cache_km, cache_vm, jnp.asarray(sel_np, BF16), jnp.asarray(pick_np, BF16))


def _dec_tail_kernel(z_ref, oa_ref, om_ref, state_ref, x_ref, wpool_ref, pscale_ref, wout_ref,
                     npost_ref, npre_ref, xo_ref, ho_ref, *, past_len):
    z = lambda g: z_ref[g]
    u = z(G_U)
    d = u.shape[1]
    gw = d // N_POOL
    dms = []
    for g, w in enumerate(POOL_WINDOWS):
        sl = slice(g * gw, (g + 1) * gw)
        tot = u[:, sl]
        for t in range(1, w):
            tot = tot + state_ref[POOL_STATE - t, :, sl]
        dms.append(tot * (1.0 / min(w, past_len + 1)) - u[:, sl])
    ob = jnp.concatenate([_dot(dms[g].astype(BF16), wpool_ref[g]) for g in range(N_POOL)], axis=1)
    ob = ob * pscale_ref[...]
    hmix = _merge(oa_ref[...], z(G_GA), ob, z(G_GB), om_ref[...], z(G_GM), z(G_ZA), z(G_ZB), z(G_ZM))
    y = _dot(hmix.astype(BF16), wout_ref[...])
    xn = x_ref[...] + _rms(y) * npost_ref[...]
    xo_ref[...] = xn
    ho_ref[...] = (_rms(xn) * npre_ref[...]).astype(ho_ref.dtype)


def _dec_tail(z, oa, om, state_t, x2, wpool, pscale, wout, npost, npre_next, past_len):
    b, d = x2.shape
    return pl.pallas_call(
        functools.partial(_dec_tail_kernel, past_len=past_len),
        out_shape=[jax.ShapeDtypeStruct((b, d), F32),
                   jax.ShapeDtypeStruct((b, d), BF16)],
        compiler_params=pltpu.CompilerParams(vmem_limit_bytes=48 * 1024 * 1024),
        name="dec_tail",
    )(z, oa, om, state_t, x2, wpool, pscale, wout, npost, npre_next)


def _stack_in_proj(w_in):
    d = w_in.shape[1]
    o = 0
    parts = {}
    for name, width in (("q", d), ("k", d), ("v", d), ("f", H_A), ("ga", d), ("u", d), ("gb", d),
                        ("qm", d), ("gm", d), ("za", d), ("zb", d), ("zm", d)):
        parts[name] = w_in[:, :, o:o + width]
        o += width
    order = ("k", "v", "u", "q", "ga", "gb", "qm", "gm", "za", "zb", "zm")
    wg = jnp.stack([parts[n] for n in order], axis=1).astype(BF16)
    wf = jnp.pad(parts["f"], ((0, 0), (0, 0), (0, LANES - H_A))).astype(BF16)
    return wg, wf


def kernel(x_prompt, x_sample, mem_prompt, cache_k_attn, cache_v_attn, cache_logf_attn, cache_k_mem,
           cache_v_mem, state_pool, page_table, norm_pre, w_in, b_forget, w_pool, pool_scale, norm_mem,
           w_mem_kv, w_out, norm_post):
    bp, sp, d = x_prompt.shape
    bd, sd, _ = x_sample.shape
    assert sd == 1 and d == H_A * DH_A
    depth = w_in.shape[0]
    nmem = mem_prompt.shape[1]
    mp = bp * sp

    wg, wf = _stack_in_proj(w_in)
    bfp = jnp.pad(b_forget, ((0, 0), (0, LANES - H_A)))[:, None, :]
    wpool = w_pool.astype(BF16)
    wout = w_out.astype(BF16)
    wmem = jnp.stack(jnp.split(w_mem_kv, 2, axis=-1), axis=1).astype(BF16)
    npre = jnp.concatenate([norm_pre, norm_pre[:1]], axis=0)[:, None, :]
    npost = norm_post[:, None, :]
    pscale = pool_scale[:, None, :]
    logf_t = jnp.swapaxes(cache_logf_attn, 2, 3)
    ckm = cache_k_mem.reshape(depth, bd, nmem, d)
    cvm = cache_v_mem.reshape(depth, bd, nmem, d)
    state_t = jnp.swapaxes(state_pool, 1, 2)
    past_len = page_table.shape[1] * cache_k_attn.shape[2]

    tm_proj = min(1024, mp)
    tm_merge = min(256, sp)
    tq = min(512, sp)

    xp = x_prompt
    xs = x_sample.reshape(bd, d)
    hp = _rmsnorm_bf16(xp.reshape(mp, d), norm_pre[0], min(512, mp))
    hs = _rmsnorm_bf16(xs, norm_pre[0], bd)
    mem2 = mem_prompt.reshape(bp * nmem, d)

    outs = {k: [] for k in ("kp", "vp", "fp", "kmp", "vmp", "pp", "kd", "vd", "fd", "pd")}
    for l in range(depth):
        kvu32, kvu16 = _project(hp, wg[l], 0, N_F32_GROUPS, (F32, BF16), tm_proj)
        (zb8,) = _project(hp, wg[l], N_F32_GROUPS, 11 - N_F32_GROUPS, (BF16,), tm_proj)
        logf, csum = _forget_gate(hp, wf[l], bfp[l], bp, sp, min(512, sp))
        c3 = csum.reshape(bp, sp, LANES)[:, :, :H_A]
        c_col = jnp.swapaxes(c3, 1, 2)[..., None]
        c_row = jnp.swapaxes(c3, 1, 2).reshape(bp, H_A, sp // tq, 1, tq)
        kvu16 = kvu16.reshape(N_F32_GROUPS, bp, sp, d)
        zb8 = zb8.reshape(11 - N_F32_GROUPS, bp, sp, d)
        oa = _fox_prompt(zb8, kvu16, c_col, c_row, tq, tq)

        hm = _rmsnorm_bf16(mem2, norm_mem[l], min(512, bp * nmem))
        kvm32, kvm16 = _project(hm, wmem[l], 0, 2, (F32, BF16), min(512, bp * nmem))
        kvu32 = kvu32.reshape(N_F32_GROUPS, bp, sp, d)
        xp, hp3 = _prompt_merge(oa, zb8, kvu32, xp, kvm16.reshape(2, bp, nmem, d), wpool[l], pscale[l],
                                wout[l], npost[l], npre[l + 1], tm_merge)
        hp = hp3.reshape(mp, d)
        outs["kp"].append(kvu32[G_K].reshape(bp, sp, H_A, DH_A))
        outs["vp"].append(kvu32[G_V].reshape(bp, sp, H_A, DH_A))
        outs["fp"].append(logf.reshape(bp, sp, LANES)[:, :, :H_A])
        outs["kmp"].append(kvm32[0].reshape(bp, nmem, H_M, d // H_M))
        outs["vmp"].append(kvm32[1].reshape(bp, nmem, H_M, d // H_M))
        outs["pp"].append(kvu32[G_U][:, sp - POOL_STATE:])

        (zs,) = _project(hs, wg[l], 0, 11, (F32,), bd)
        lfnew = _dec_forget_gate(hs, wf[l], bfp[l])
        lfnew_rep = jnp.broadcast_to(lfnew[:, :H_A, None], (bd, H_A, LANES))
        bias5 = _dec_bias(page_table, lfnew_rep, logf_t, l)
        q3 = zs[G_Q].reshape(bd, H_A, DH_A)
        kn3 = zs[G_K].reshape(bd, H_A, DH_A)
        vn3 = zs[G_V].reshape(bd, H_A, DH_A)
        oa_s = _dec_attn(page_table, q3, kn3, vn3, bias5, cache_k_attn, cache_v_attn, l, 4)
        om_s = _dec_mem_attn(zs[G_QM][:, None, :], ckm, cvm, l)
        xs, hs = _dec_tail(zs, oa_s.reshape(bd, d), om_s.reshape(bd, d), state_t[l], xs,
                           wpool[l], pscale[l], wout[l], npost[l], npre[l + 1], past_len)
        st_new = jnp.concatenate([state_pool[l][:, 1:], zs[G_U][:, None, :]], axis=1)
        outs["kd"].append(kn3.reshape(bd, 1, H_A, DH_A))
        outs["vd"].append(vn3.reshape(bd, 1, H_A, DH_A))
        outs["fd"].append(lfnew[:, None, :H_A])
        outs["pd"].append(st_new)

    st = {k: jnp.stack(v) for k, v in outs.items()}
    return (xp, xs.reshape(bd, 1, d), st["kp"], st["vp"], st["fp"], st["kmp"], st["vmp"], st["pp"],
            st["kd"], st["vd"], st["fd"], st["pd"])
```

```python
import functools

import jax
import jax.numpy as jnp
import numpy as np
from jax import lax
from jax.experimental import pallas as pl
from jax.experimental.pallas import tpu as pltpu

F32 = jnp.float32
BF16 = jnp.bfloat16

RMS_EPS = 1e-6
H_A = 8
DH_A = 128
N_POOL = 4
POOL_WINDOWS = (2, 4, 8, 16)
POOL_STATE = 15
H_M = 4
NEG = -1e30
LOG2E = 1.4426950408889634

LANES = 128
SUBLANES = 8

G_K, G_V, G_U, G_Q, G_GA, G_GB, G_QM, G_GM, G_ZA, G_ZB, G_ZM = range(11)
N_F32_GROUPS = 3
N_EXT = 3


def _cparams(sem, vmem_mb):
    return pltpu.CompilerParams(dimension_semantics=sem, vmem_limit_bytes=vmem_mb * 1024 * 1024)


def _rms(x):
    return x * lax.rsqrt(jnp.mean(x * x, axis=-1, keepdims=True) + RMS_EPS)


def _sigmoid(x):
    return 1.0 / (1.0 + jnp.exp(-x))


def _log_sigmoid(x):
    return jnp.minimum(x, 0.0) - jnp.log1p(jnp.exp(-jnp.abs(x)))


def _split3(x):
    hi = x.astype(BF16)
    r1 = x - hi.astype(F32)
    mid = r1.astype(BF16)
    lo = (r1 - mid.astype(F32)).astype(BF16)
    return hi, mid, lo


def _dot(a, b):
    return jnp.dot(a, b, preferred_element_type=F32)


def _dot_nt(a, b):
    return lax.dot_general(a, b, (((1,), (1,)), ((), ())), preferred_element_type=F32)


def _norm_kernel(x_ref, g_ref, o_ref):
    o_ref[...] = (_rms(x_ref[...]) * g_ref[...]).astype(o_ref.dtype)


def _rmsnorm_bf16(x2d, gain, tm):
    m, d = x2d.shape
    return pl.pallas_call(
        _norm_kernel,
        grid=(m // tm,),
        in_specs=[pl.BlockSpec((tm, d), lambda i: (i, 0)),
                  pl.BlockSpec((1, d), lambda i: (0, 0))],
        out_specs=pl.BlockSpec((tm, d), lambda i: (i, 0)),
        out_shape=jax.ShapeDtypeStruct((m, d), BF16),
        compiler_params=_cparams(("arbitrary",), 32),
        name="rmsnorm",
    )(x2d, gain.reshape(1, d))


def _proj_kernel(h_ref, w_ref, *o_refs, first_scale):
    acc = _dot(h_ref[...], w_ref[...])
    if first_scale != 1.0:
        acc = acc * jnp.where(pl.program_id(0) == 0, first_scale, 1.0)
    for o in o_refs:
        o[...] = acc.astype(o.dtype)


def _project(h2d, w_groups, g0, ng, out_dtypes, tm, first_scale=1.0):
    m, d = h2d.shape
    n = w_groups.shape[-1]
    return pl.pallas_call(
        functools.partial(_proj_kernel, first_scale=first_scale),
        grid=(ng, m // tm),
        in_specs=[pl.BlockSpec((tm, d), lambda g, i: (i, 0)),
                  pl.BlockSpec((None, d, n), lambda g, i: (g + g0, 0, 0))],
        out_specs=[pl.BlockSpec((None, tm, n), lambda g, i: (g, i, 0)) for _ in out_dtypes],
        out_shape=[jax.ShapeDtypeStruct((ng, m, n), dt) for dt in out_dtypes],
        compiler_params=_cparams(("arbitrary", "arbitrary"), 48),
        name="in_proj",
    )(h2d, w_groups)


def _proj_cache_kernel(h_ref, w_ref, *rest):
    o5_ref, o16_ref = rest[-2:]
    acc = _dot(h_ref[...], w_ref[...])
    o16_ref[...] = acc.astype(o16_ref.dtype)
    for h in range(H_A):
        o5_ref[:, h, :] = acc[:, h * DH_A:(h + 1) * DH_A]


def _project_cache(h3, w_groups, g, stacked, layer, depth, tm):
    batch, seq, d = h3.shape
    in_specs = [pl.BlockSpec((None, tm, d), lambda b, i: (b, i, 0)),
                pl.BlockSpec((None, d, d), lambda b, i: (g, 0, 0))]
    args = [h3, w_groups]
    aliases = {}
    if stacked is not None:
        in_specs.append(pl.BlockSpec(memory_space=pl.ANY))
        args.append(stacked)
        aliases = {2: 0}
    return pl.pallas_call(
        _proj_cache_kernel,
        grid=(batch, seq // tm),
        in_specs=in_specs,
        out_specs=[pl.BlockSpec((None, None, tm, H_A, DH_A), lambda b, i: (layer, b, i, 0, 0)),
                   pl.BlockSpec((None, tm, d), lambda b, i: (b, i, 0))],
        out_shape=[jax.ShapeDtypeStruct((depth, batch, seq, H_A, DH_A), F32),
                   jax.ShapeDtypeStruct((batch, seq, d), BF16)],
        input_output_aliases=aliases,
        compiler_params=_cparams(("arbitrary", "arbitrary"), 48),
        name="kv_proj",
    )(*args)


def _fgate_kernel(h_ref, wf_ref, bf_ref, tri_ref, pq_ref, pk_ref, oq_ref, ok_ref,
                  logf_ref, qx_ref, kx_ref, carry_ref):
    @pl.when(pl.program_id(1) == 0)
    def _():
        carry_ref[...] = jnp.zeros_like(carry_ref)

    lf = _log_sigmoid(_dot(h_ref[...], wf_ref[...]) + bf_ref[...])
    logf_ref[...] = lf
    tri = tri_ref[...]
    hi, mid, lo = _split3(lf)
    c = (_dot(tri, hi) + _dot(tri, mid)) + _dot(tri, lo) + carry_ref[0:1, :]
    tm = c.shape[0]
    carry_ref[...] = jnp.broadcast_to(c[tm - 1:tm, :], carry_ref.shape)
    x = jnp.concatenate(_split3(c * LOG2E), axis=1)
    for h in range(H_A):
        qx_ref[h] = (_dot(x, pq_ref[h]) + oq_ref[...]).astype(qx_ref.dtype)
        kx_ref[h] = (_dot(x, pk_ref[h]) + ok_ref[...]).astype(kx_ref.dtype)


def _forget_gate(h2d, wf, bf, batch, seq, tm):
    d = h2d.shape[1]
    ns = seq // tm
    tri = jnp.asarray(np.tril(np.ones((tm, tm), np.float32)), BF16)
    pq = np.zeros((H_A, N_EXT * LANES, LANES), np.float32)
    pk = np.zeros((H_A, N_EXT * LANES, LANES), np.float32)
    oq = np.zeros((1, LANES), np.float32)
    ok = np.zeros((1, LANES), np.float32)
    for j in range(N_EXT):
        for h in range(H_A):
            pq[h, j * LANES + h, j] = 1.0
            pk[h, j * LANES + h, N_EXT + j] = -1.0
        oq[0, N_EXT + j] = 1.0
        ok[0, j] = 1.0
    const = lambda shape: pl.BlockSpec(shape, lambda b, i: (0,) * len(shape))
    return pl.pallas_call(
        _fgate_kernel,
        grid=(batch, ns),
        in_specs=[pl.BlockSpec((tm, d), lambda b, i: (b * ns + i, 0)),
                  const((d, LANES)), const((1, LANES)), const((tm, tm)),
                  const(pq.shape), const(pk.shape), const((1, LANES)), const((1, LANES))],
        out_specs=[pl.BlockSpec((tm, LANES), lambda b, i: (b * ns + i, 0)),
                   pl.BlockSpec((None, H_A, tm, LANES), lambda b, i: (b, 0, i, 0)),
                   pl.BlockSpec((None, H_A, tm, LANES), lambda b, i: (b, 0, i, 0))],
        out_shape=[jax.ShapeDtypeStruct((batch * seq, LANES), F32),
                   jax.ShapeDtypeStruct((batch, H_A, seq, LANES), BF16),
                   jax.ShapeDtypeStruct((batch, H_A, seq, LANES), BF16)],
        scratch_shapes=[pltpu.VMEM((SUBLANES, LANES), F32)],
        compiler_params=_cparams(("arbitrary", "arbitrary"), 32),
        name="forget_gate",
    )(h2d, wf, bf, tri, jnp.asarray(pq, BF16), jnp.asarray(pk, BF16), jnp.asarray(oq), jnp.asarray(ok))


def _flash_kernel(q_ref, qx_ref, k_ref, kx_ref, v_ref, o_ref, m_ref, acc_ref, s_ref, *, tq, tk, ts):
    qi = pl.program_id(2)
    nsub = tq // ts
    kper = tq // tk
    assert kper == 2, "the two logit buffers are indexed statically: two key blocks per q tile"
    m_ref[...] = jnp.full(m_ref.shape, NEG, F32)
    acc_ref[...] = jnp.zeros_like(acc_ref)
    ones = jnp.ones((tk, DH_A), BF16)

    def logits(kj, buf):
        ks = pl.multiple_of(kj * tk, tk)
        k = jnp.concatenate([k_ref[pl.ds(ks, tk), :], kx_ref[pl.ds(ks, tk), :]], axis=1)
        q = jnp.concatenate([q_ref[...], qx_ref[...]], axis=1)
        s_ref[buf] = _dot_nt(q, k)

    def softmax_pv(kj, buf, chains):
        ks = pl.multiple_of(kj * tk, tk)
        v = jnp.concatenate([v_ref[pl.ds(ks, tk), :], ones], axis=1)
        for r, col_minus_row in chains:
            rows = slice(r * ts, (r + 1) * ts)
            s = s_ref[buf, rows, :]
            if col_minus_row is not None:
                rel = (lax.broadcasted_iota(jnp.int32, (ts, tk), 1)
                       - lax.broadcasted_iota(jnp.int32, (ts, tk), 0))
                s = jnp.where(rel + col_minus_row <= 0, s, NEG)
            m_old = m_ref[rows, :]
            m_new = jnp.maximum(m_old, jnp.max(s, axis=-1, keepdims=True))
            p = jnp.exp2(s - m_new)
            acc_ref[rows, :] = jnp.exp2(m_old - m_new) * acc_ref[rows, :] + _dot(p.astype(BF16), v)
            m_ref[rows, :] = m_new

    all_chains = [(r, None) for r in range(nsub)]

    def pair(it, carry):
        j = it * kper
        logits(j + 1, 1)
        softmax_pv(j, 0, all_chains)
        logits(j + 2, 0)
        softmax_pv(j + 1, 1, all_chains)
        return carry

    logits(0, 0)
    lax.fori_loop(0, qi, pair, 0)
    n_full = qi * kper
    for d in range(kper):
        if d + 1 < kper:
            logits(n_full + d + 1, d + 1)
        chains = []
        for r in range(nsub):
            if d * tk > r * ts + ts - 1:
                continue
            fully_visible = d * tk + tk - 1 <= r * ts
            chains.append((r, None if fully_visible else d * tk - r * ts))
        softmax_pv(n_full + d, d, chains)
    acc = acc_ref[...]
    o_ref[...] = (acc[:, :DH_A] / acc[:, DH_A:]).astype(o_ref.dtype)


def _fox_prompt(q4, qx, k16, kx, v16, tq, tk, ts):
    _, batch, seq, d = q4.shape
    kern = functools.partial(_flash_kernel, tq=tq, tk=tk, ts=ts)
    return pl.pallas_call(
        kern,
        grid=(batch, H_A, seq // tq),
        in_specs=[pl.BlockSpec((None, None, tq, DH_A), lambda b, h, i: (G_Q - N_F32_GROUPS, b, i, h)),
                  pl.BlockSpec((None, None, tq, LANES), lambda b, h, i: (b, h, i, 0)),
                  pl.BlockSpec((None, seq, DH_A), lambda b, h, i: (b, 0, h)),
                  pl.BlockSpec((None, None, seq, LANES), lambda b, h, i: (b, h, 0, 0)),
                  pl.BlockSpec((None, seq, DH_A), lambda b, h, i: (b, 0, h))],
        out_specs=pl.BlockSpec((None, tq, DH_A), lambda b, h, i: (b, i, h)),
        out_shape=jax.ShapeDtypeStruct((batch, seq, d), BF16),
        scratch_shapes=[pltpu.VMEM((tq, 1), F32), pltpu.VMEM((tq, 2 * DH_A), F32),
                        pltpu.VMEM((2, tq, tk), F32)],
        compiler_params=_cparams(("arbitrary", "arbitrary", "arbitrary"), 48),
        name="fox_prompt",
    )(q4, qx, k16, kx, v16)


def _pool_means_minus_u(ext_ref, s_ref, u, inv_cnt, tm):
    gw = u.shape[1] // N_POOL
    n = tm + 15
    s_ref[0:n, :] = ext_ref[1:1 + n, :] + ext_ref[0:n, :]
    outs = [s_ref[15:15 + tm, 0:gw]]
    n = tm + 13
    s_ref[0:n, gw:] = s_ref[2:2 + n, gw:] + s_ref[0:n, gw:]
    outs.append(s_ref[13:13 + tm, gw:2 * gw])
    n = tm + 9
    s_ref[0:n, 2 * gw:] = s_ref[4:4 + n, 2 * gw:] + s_ref[0:n, 2 * gw:]
    outs.append(s_ref[9:9 + tm, 2 * gw:3 * gw])
    n = tm + 1
    s_ref[0:n, 3 * gw:] = s_ref[8:8 + n, 3 * gw:] + s_ref[0:n, 3 * gw:]
    outs.append(s_ref[1:1 + tm, 3 * gw:])
    return [outs[g] * inv_cnt[g] - u[:, g * gw:(g + 1) * gw] for g in range(N_POOL)]


def _merge(oa, ga, ob, gb, om, gm, za, zb, zm):
    return (_sigmoid(za) * oa * (ga * _sigmoid(ga))
            + _sigmoid(zb) * ob * (gb * _sigmoid(gb))
            + _sigmoid(zm) * om * (gm * _sigmoid(gm)))


def _pmerge_kernel(oa_ref, ga_ref, gb_ref, qm_ref, gm_ref, za_ref, zb_ref, zm_ref, u_ref, x_ref,
                   km_ref, vm_ref, wpool_ref, pscale_ref, wout_ref, npost_ref, npre_ref,
                   xo_ref, ho_ref, ext_ref, s_ref, *, tm):
    i = pl.program_id(1)

    @pl.when(i == 0)
    def _():
        ext_ref[0:16, :] = jnp.zeros((16, ext_ref.shape[1]), F32)

    u = u_ref[...]
    d = u.shape[1]
    ext_ref[16:16 + tm, :] = u
    pos = i * tm + lax.broadcasted_iota(jnp.int32, (tm, 1), 0)
    inv_cnt = [1.0 / jnp.minimum(w, pos + 1).astype(F32) for w in POOL_WINDOWS]
    dm = _pool_means_minus_u(ext_ref, s_ref, u, inv_cnt, tm)
    ob = jnp.concatenate([_dot(dm[g].astype(BF16), wpool_ref[g]) for g in range(N_POOL)], axis=1)
    ob = ob * pscale_ref[...]
    ext_ref[0:16, :] = ext_ref[tm:tm + 16, :]

    dhm = d // H_M
    oms = []
    for hm in range(H_M):
        sl = slice(hm * dhm, (hm + 1) * dhm)
        s = _dot_nt(qm_ref[:, sl], km_ref[:, sl]) * (dhm ** -0.5)
        p = jnp.exp(s - jnp.max(s, axis=-1, keepdims=True))
        den = jnp.sum(p, axis=-1, keepdims=True)
        oms.append(_dot(p.astype(BF16), vm_ref[:, sl]) / den)
    om = jnp.concatenate(oms, axis=1)

    f = lambda r: r[...].astype(F32)
    hmix = _merge(f(oa_ref), f(ga_ref), ob, f(gb_ref), om, f(gm_ref), f(za_ref), f(zb_ref), f(zm_ref))
    y = _dot(hmix.astype(BF16), wout_ref[...])
    xn = x_ref[...] + _rms(y) * npost_ref[...]
    xo_ref[...] = xn
    ho_ref[...] = (_rms(xn) * npre_ref[...]).astype(ho_ref.dtype)


def _prompt_merge(oa, zb8, u32, x3, kvm, wpool, pscale, wout, npost, npre_next, tm):
    batch, seq, d = x3.shape
    nmem = kvm.shape[2]

    def grp(g):
        return pl.BlockSpec((None, None, tm, d), lambda b, i: (g - N_F32_GROUPS, b, i, 0))

    row = pl.BlockSpec((None, tm, d), lambda b, i: (b, i, 0))
    vec = pl.BlockSpec((1, d), lambda b, i: (0, 0))
    kern = functools.partial(_pmerge_kernel, tm=tm)
    return pl.pallas_call(
        kern,
        grid=(batch, seq // tm),
        in_specs=[row, grp(G_GA), grp(G_GB), grp(G_QM), grp(G_GM), grp(G_ZA), grp(G_ZB), grp(G_ZM),
                  row, row,
                  pl.BlockSpec((None, None, nmem, d), lambda b, i: (0, b, 0, 0)),
                  pl.BlockSpec((None, None, nmem, d), lambda b, i: (1, b, 0, 0)),
                  pl.BlockSpec(wpool.shape, lambda b, i: (0, 0, 0)),
                  vec,
                  pl.BlockSpec((d, d), lambda b, i: (0, 0)),
                  vec, vec],
        out_specs=[row, row],
        out_shape=[jax.ShapeDtypeStruct((batch, seq, d), F32),
                   jax.ShapeDtypeStruct((batch, seq, d), BF16)],
        scratch_shapes=[pltpu.VMEM((tm + 16, d), F32), pltpu.VMEM((tm + 16, d), F32)],
        compiler_params=_cparams(("arbitrary", "arbitrary"), 56),
        name="prompt_merge",
    )(oa, zb8, zb8, zb8, zb8, zb8, zb8, zb8, u32, x3, kvm, kvm, wpool, pscale, wout, npost, npre_next)


def _dec_fgate_kernel(h_ref, wf_ref, bf_ref, o_ref):
    o_ref[...] = _log_sigmoid(_dot(h_ref[...], wf_ref[...]) + bf_ref[...])


def _dec_forget_gate(h2d, wf, bf):
    m = h2d.shape[0]
    return pl.pallas_call(
        _dec_fgate_kernel,
        out_shape=jax.ShapeDtypeStruct((m, LANES), F32),
        name="dec_forget_gate",
    )(h2d, wf, bf)


def _dec_bias_kernel(pt_ref, lfnew_ref, sufm_ref, pgm_ref, *rest, npages):
    page_refs = rest[:npages]
    o_ref = rest[npages]
    x = jnp.concatenate([r[...] for r in page_refs], axis=0)
    hi, mid, lo = _split3(x)
    sufm = sufm_ref[...]
    within = (_dot(hi, sufm) + _dot(mid, sufm)) + _dot(lo, sufm)
    ones = jnp.ones((LANES, LANES), BF16)
    tot = (_dot(hi, ones) + _dot(mid, ones)) + _dot(lo, ones)
    th, tmid, tl = _split3(tot)
    pgm = pgm_ref[...]
    later = (_dot(pgm, th) + _dot(pgm, tmid)) + _dot(pgm, tl)
    lfnew = lfnew_ref[...]
    bias = (within + later).reshape(npages, H_A, LANES) + lfnew[None]
    bias = bias.reshape(npages * H_A, LANES) * LOG2E

    bh = bias.astype(BF16).astype(F32)
    bl = (bias - bh).astype(BF16).astype(F32)
    half = LANES // 2
    lane = lax.broadcasted_iota(jnp.int32, bias.shape, 1)
    first = jnp.where(lane < half, bh, pltpu.roll(bl, half, 1))
    second = jnp.where(lane < half, pltpu.roll(bh, half, 1), bl)
    first = first.reshape(npages, 1, H_A, LANES)
    second = second.reshape(npages, 1, H_A, LANES)
    halves = jnp.concatenate([first, second], axis=1)
    o_ref[...] = jnp.concatenate([halves, halves], axis=2).astype(o_ref.dtype)


def _dec_bias(page_table, lfnew_rep, logf_t, layer):
    batch, npages = page_table.shape
    page = logf_t.shape[-1]
    assert page == LANES
    sufm = jnp.asarray(np.tril(np.ones((page, page), np.float32), -1), BF16)
    pg = np.arange(npages * H_A)
    pgm_np = ((pg[None, :] // H_A > pg[:, None] // H_A) & (pg[None, :] % H_A == pg[:, None] % H_A))
    pgm = jnp.asarray(pgm_np.astype(np.float32), BF16)

    def page_spec(j):
        return pl.BlockSpec((None, None, H_A, LANES), lambda b, pt: (layer, pt[b, j], 0, 0))

    kern = functools.partial(_dec_bias_kernel, npages=npages)
    grid_spec = pltpu.PrefetchScalarGridSpec(
        num_scalar_prefetch=1,
        grid=(batch,),
        in_specs=[pl.BlockSpec((None, H_A, LANES), lambda b, pt: (b, 0, 0)),
                  pl.BlockSpec((page, page), lambda b, pt: (0, 0)),
                  pl.BlockSpec(pgm.shape, lambda b, pt: (0, 0))]
                 + [page_spec(j) for j in range(npages)],
        out_specs=pl.BlockSpec((None, npages, 2, 2 * H_A, LANES), lambda b, pt: (b, 0, 0, 0, 0)),
    )
    return pl.pallas_call(
        kern,
        grid_spec=grid_spec,
        out_shape=jax.ShapeDtypeStruct((batch, npages, 2, 2 * H_A, LANES), BF16),
        compiler_params=_cparams(("arbitrary",), 32),
        name="dec_bias",
    )(page_table, lfnew_rep, sufm, pgm, *([logf_t] * npages))


def _dec_attn_kernel(pt_ref, q_ref, kn_ref, vn_ref, bias_ref, mask_ref, *rest, ppb, scale):
    k_refs = rest[:ppb]
    v_refs = rest[ppb:2 * ppb]
    o_ref = rest[2 * ppb]
    m_ref, l_ref, acc_ref = rest[2 * ppb + 1:]
    j = pl.program_id(1)

    @pl.when(j == 0)
    def _():
        m_ref[...] = jnp.full(m_ref.shape, NEG, F32)
        l_ref[...] = jnp.zeros_like(l_ref)
        acc_ref[...] = jnp.zeros_like(acc_ref)

    qs = q_ref[...] * (scale * LOG2E)
    ones = jnp.ones((2 * LANES, LANES), BF16)
    page = k_refs[0].shape[0]
    half = page // 2
    m, l, acc = m_ref[...], l_ref[...], acc_ref[...]
    for r in range(ppb):
        k = k_refs[r][...]
        prod = (k * qs[None]).reshape(page * H_A, DH_A).astype(BF16)
        lbs = []
        for hh in range(2):
            t2 = bias_ref[r, hh]
            lbs.append((t2[None] * mask_ref[...]).reshape(half * H_A, LANES))
        lhs = jnp.concatenate([prod, jnp.concatenate(lbs, axis=0)], axis=1)
        s = _dot(lhs, ones).reshape(page, H_A, LANES)
        m_new = jnp.maximum(m, jnp.max(s, axis=0))
        p = jnp.exp2(s - m_new[None])
        alpha = jnp.exp2(m - m_new)
        l = alpha * l + jnp.sum(p, axis=0)
        acc = alpha * acc + jnp.sum(p * v_refs[r][...], axis=0)
        m = m_new
    m_ref[...], l_ref[...], acc_ref[...] = m, l, acc

    @pl.when(j == pl.num_programs(1) - 1)
    def _():
        s_new = jnp.sum(qs * kn_ref[...], axis=-1, keepdims=True)
        m_fin = jnp.maximum(m, s_new)
        p_new = jnp.exp2(s_new - m_fin)
        a = jnp.exp2(m - m_fin)
        o_ref[...] = (a * acc + p_new * vn_ref[...]) / (a * l + p_new)


def _dec_attn(page_table, q3, kn3, vn3, bias5, cache_k, cache_v, layer, ppb):
    batch, npages = page_table.shape
    page = cache_k.shape[2]
    half = page // 2
    mask_np = np.zeros((half // 2, 2 * H_A, LANES), np.float32)
    for i in range(half // 2):
        for s in range(2):
            pos = 2 * i + s
            mask_np[i, s * H_A:(s + 1) * H_A, pos] = 1.0
            mask_np[i, s * H_A:(s + 1) * H_A, half + pos] = 1.0
    mask = jnp.asarray(mask_np, BF16)
    vec = pl.BlockSpec((None, H_A, DH_A), lambda b, j, pt: (b, 0, 0))

    def page_spec(r):
        return pl.BlockSpec((None, None, page, H_A, DH_A),
                            lambda b, j, pt: (layer, pt[b, j * ppb + r], 0, 0, 0))

    kern = functools.partial(_dec_attn_kernel, ppb=ppb, scale=DH_A ** -0.5)
    grid_spec = pltpu.PrefetchScalarGridSpec(
        num_scalar_prefetch=1,
        grid=(batch, npages // ppb),
        in_specs=[vec, vec, vec,
                  pl.BlockSpec((None, ppb, 2, 2 * H_A, LANES), lambda b, j, pt: (b, j, 0, 0, 0)),
                  pl.BlockSpec(mask.shape, lambda b, j, pt: (0, 0, 0))]
                 + [page_spec(r) for r in range(ppb)] * 2,
        out_specs=vec,
        scratch_shapes=[pltpu.VMEM((H_A, DH_A), F32)] * 3,
    )
    return pl.pallas_call(
        kern,
        grid_spec=grid_spec,
        out_shape=jax.ShapeDtypeStruct((batch, H_A, DH_A), F32),
        compiler_params=_cparams(("arbitrary", "arbitrary"), 48),
        name="dec_attn",
    )(page_table, q3, kn3, vn3, bias5, mask, *([cache_k] * ppb), *([cache_v] * ppb))


def _dec_mem_kernel(q_ref, k_ref, v_ref, sel_ref, selt_ref, o_ref, *, scale):
    q = q_ref[...] * scale
    prod = (k_ref[...] * q).astype(BF16)
    s = _dot(prod, sel_ref[...])
    p = jnp.exp(s - jnp.max(s, axis=0, keepdims=True))
    pe = _dot(p.astype(BF16), selt_ref[...])
    num = jnp.sum(pe * v_ref[...], axis=0, keepdims=True)
    o_ref[...] = num / jnp.sum(pe, axis=0, keepdims=True)


def _dec_mem_attn(qm, cache_km, cache_vm, layer):
    batch, _, d = qm.shape
    nmem = cache_km.shape[2]
    dhm = d // H_M
    rep = LANES // H_M
    sel_np = (np.arange(d)[:, None] // dhm == np.arange(LANES)[None, :] // rep).astype(np.float32)
    pick_np = (np.arange(LANES)[:, None] == (np.arange(d)[None, :] // dhm) * rep).astype(np.float32)
    kern = functools.partial(_dec_mem_kernel, scale=dhm ** -0.5)
    return pl.pallas_call(
        kern,
        grid=(batch,),
        in_specs=[pl.BlockSpec((None, 1, d), lambda b: (b, 0, 0)),
                  pl.BlockSpec((None, None, nmem, d), lambda b: (layer, b, 0, 0)),
                  pl.BlockSpec((None, None, nmem, d), lambda b: (layer, b, 0, 0)),
                  pl.BlockSpec((d, LANES), lambda b: (0, 0)),
                  pl.BlockSpec((LANES, d), lambda b: (0, 0))],
        out_specs=pl.BlockSpec((None, 1, d), lambda b: (b, 0, 0)),
        out_shape=jax.ShapeDtypeStruct((batch, 1, d), F32),
        compiler_params=_cparams(("arbitrary",), 32),
        name="dec_mem_attn",
    )(qm, cache_km, cache_vm, jnp.asarray(sel_np, BF16), jnp.asarray(pick_np, BF16))


def _dec_tail_kernel(z_ref, oa_ref, om_ref, state_ref, x_ref, wpool_ref, pscale_ref, wout_ref,
                     npost_ref, npre_ref, xo_ref, ho_ref, *, past_len):
    z = lambda g: z_ref[g]
    u = z(G_U)
    d = u.shape[1]
    gw = d // N_POOL
    dms = []
    for g, w in enumerate(POOL_WINDOWS):
        sl = slice(g * gw, (g + 1) * gw)
        tot = u[:, sl]
        for t in range(1, w):
            tot = tot + state_ref[POOL_STATE - t, :, sl]
        dms.append(tot * (1.0 / min(w, past_len + 1)) - u[:, sl])
    ob = jnp.concatenate([_dot(dms[g].astype(BF16), wpool_ref[g]) for g in range(N_POOL)], axis=1)
    ob = ob * pscale_ref[...]
    hmix = _merge(oa_ref[...], z(G_GA), ob, z(G_GB), om_ref[...], z(G_GM), z(G_ZA), z(G_ZB), z(G_ZM))
    y = _dot(hmix.astype(BF16), wout_ref[...])
    xn = x_ref[...] + _rms(y) * npost_ref[...]
    xo_ref[...] = xn
    ho_ref[...] = (_rms(xn) * npre_ref[...]).astype(ho_ref.dtype)


def _dec_tail(z, oa, om, state_t, x2, wpool, pscale, wout, npost, npre_next, past_len):
    b, d = x2.shape
    return pl.pallas_call(
        functools.partial(_dec_tail_kernel, past_len=past_len),
        out_shape=[jax.ShapeDtypeStruct((b, d), F32),
                   jax.ShapeDtypeStruct((b, d), BF16)],
        compiler_params=pltpu.CompilerParams(vmem_limit_bytes=48 * 1024 * 1024),
        name="dec_tail",
    )(z, oa, om, state_t, x2, wpool, pscale, wout, npost, npre_next)


def _stack_in_proj(w_in):
    d = w_in.shape[1]
    o = 0
    parts = {}
    for name, width in (("q", d), ("k", d), ("v", d), ("f", H_A), ("ga", d), ("u", d), ("gb", d),
                        ("qm", d), ("gm", d), ("za", d), ("zb", d), ("zm", d)):
        parts[name] = w_in[:, :, o:o + width]
        o += width
    order = ("k", "v", "u", "q", "ga", "gb", "qm", "gm", "za", "zb", "zm")
    wg = jnp.stack([parts[n] for n in order], axis=1).astype(BF16)
    wf = jnp.pad(parts["f"], ((0, 0), (0, 0), (0, LANES - H_A))).astype(BF16)
    return wg, wf


def kernel(x_prompt, x_sample, mem_prompt, cache_k_attn, cache_v_attn, cache_logf_attn, cache_k_mem,
           cache_v_mem, state_pool, page_table, norm_pre, w_in, b_forget, w_pool, pool_scale, norm_mem,
           w_mem_kv, w_out, norm_post):
    bp, sp, d = x_prompt.shape
    bd, sd, _ = x_sample.shape
    assert sd == 1 and d == H_A * DH_A
    depth = w_in.shape[0]
    nmem = mem_prompt.shape[1]
    mp = bp * sp

    wg, wf = _stack_in_proj(w_in)
    bfp = jnp.pad(b_forget, ((0, 0), (0, LANES - H_A)))[:, None, :]
    wpool = w_pool.astype(BF16)
    wout = w_out.astype(BF16)
    wmem = jnp.stack(jnp.split(w_mem_kv, 2, axis=-1), axis=1).astype(BF16)
    npre = jnp.concatenate([norm_pre, norm_pre[:1]], axis=0)[:, None, :]
    npost = norm_post[:, None, :]
    pscale = pool_scale[:, None, :]
    logf_t = jnp.swapaxes(cache_logf_attn, 2, 3)
    ckm = cache_k_mem.reshape(depth, bd, nmem, d)
    cvm = cache_v_mem.reshape(depth, bd, nmem, d)
    state_t = jnp.swapaxes(state_pool, 1, 2)
    past_len = page_table.shape[1] * cache_k_attn.shape[2]

    tm_proj = min(1024, sp)
    tm_merge = min(256, sp)
    tq = min(1024, sp)
    tk = min(512, tq)
    ts = min(256, tk)
    q_scale = DH_A ** -0.5 * LOG2E

    xp = x_prompt
    xs = x_sample.reshape(bd, d)
    hp = _rmsnorm_bf16(xp.reshape(mp, d), norm_pre[0], min(512, mp))
    hs = _rmsnorm_bf16(xs, norm_pre[0], bd)
    mem2 = mem_prompt.reshape(bp * nmem, d)

    k_all = v_all = None
    outs = {k: [] for k in ("fp", "kmp", "vmp", "pp", "kd", "vd", "fd", "pd")}
    for l in range(depth):
        hp3 = hp.reshape(bp, sp, d)
        k_all, k16 = _project_cache(hp3, wg[l], G_K, k_all, l, depth, tm_proj)
        v_all, v16 = _project_cache(hp3, wg[l], G_V, v_all, l, depth, tm_proj)
        (u32,) = _project(hp, wg[l], G_U, 1, (F32,), tm_proj)
        (zb8,) = _project(hp, wg[l], N_F32_GROUPS, 11 - N_F32_GROUPS, (BF16,), tm_proj, first_scale=q_scale)
        logf, qx, kx = _forget_gate(hp, wf[l], bfp[l], bp, sp, min(512, sp))
        u32 = u32.reshape(bp, sp, d)
        zb8 = zb8.reshape(11 - N_F32_GROUPS, bp, sp, d)
        oa = _fox_prompt(zb8, qx, k16, kx, v16, tq, tk, ts)

        hm = _rmsnorm_bf16(mem2, norm_mem[l], min(512, bp * nmem))
        kvm32, kvm16 = _project(hm, wmem[l], 0, 2, (F32, BF16), min(512, bp * nmem))
        xp, hp3 = _prompt_merge(oa, zb8, u32, xp, kvm16.reshape(2, bp, nmem, d), wpool[l], pscale[l],
                                wout[l], npost[l], npre[l + 1], tm_merge)
        hp = hp3.reshape(mp, d)
        outs["fp"].append(logf.reshape(bp, sp, LANES)[:, :, :H_A])
        outs["kmp"].append(kvm32[0].reshape(bp, nmem, H_M, d // H_M))
        outs["vmp"].append(kvm32[1].reshape(bp, nmem, H_M, d // H_M))
        outs["pp"].append(u32[:, sp - POOL_STATE:])

        (zs,) = _project(hs, wg[l], 0, 11, (F32,), bd)
        lfnew = _dec_forget_gate(hs, wf[l], bfp[l])
        lfnew_rep = jnp.broadcast_to(lfnew[:, :H_A, None], (bd, H_A, LANES))
        bias5 = _dec_bias(page_table, lfnew_rep, logf_t, l)
        q3 = zs[G_Q].reshape(bd, H_A, DH_A)
        kn3 = zs[G_K].reshape(bd, H_A, DH_A)
        vn3 = zs[G_V].reshape(bd, H_A, DH_A)
        oa_s = _dec_attn(page_table, q3, kn3, vn3, bias5, cache_k_attn, cache_v_attn, l, 4)
        om_s = _dec_mem_attn(zs[G_QM][:, None, :], ckm, cvm, l)
        xs, hs = _dec_tail(zs, oa_s.reshape(bd, d), om_s.reshape(bd, d), state_t[l], xs,
                           wpool[l], pscale[l], wout[l], npost[l], npre[l + 1], past_len)
        st_new = jnp.concatenate([state_pool[l][:, 1:], zs[G_U][:, None, :]], axis=1)
        outs["kd"].append(kn3.reshape(bd, 1, H_A, DH_A))
        outs["vd"].append(vn3.reshape(bd, 1, H_A, DH_A))
        outs["fd"].append(lfnew[:, None, :H_A])
        outs["pd"].append(st_new)

    st = {k: jnp.stack(v) for k, v in outs.items()}
    return (xp, xs.reshape(bd, 1, d), k_all, v_all, st["fp"], st["kmp"], st["vmp"], st["pp"],
            st["kd"], st["vd"], st["fd"], st["pd"])
```

```python
import functools

import jax
import jax.numpy as jnp
import numpy as np
from jax import lax
from jax.experimental import pallas as pl
from jax.experimental.pallas import tpu as pltpu

F32 = jnp.float32
BF16 = jnp.bfloat16

RMS_EPS = 1e-6
H_A = 8
DH_A = 128
N_POOL = 4
POOL_WINDOWS = (2, 4, 8, 16)
POOL_STATE = 15
H_M = 4
NEG = -1e30
LOG2E = 1.4426950408889634

LANES = 128
SUBLANES = 8

G_K, G_V, G_U, G_Q, G_GA, G_GB, G_QM, G_GM, G_ZA, G_ZB, G_ZM = range(11)
N_F32_GROUPS = 3
N_EXT = 3


def _cparams(sem, vmem_mb):
    return pltpu.CompilerParams(dimension_semantics=sem, vmem_limit_bytes=vmem_mb * 1024 * 1024)


def _rms(x):
    return x * lax.rsqrt(jnp.mean(x * x, axis=-1, keepdims=True) + RMS_EPS)


def _log_sigmoid(x):
    return jnp.minimum(x, 0.0) - jnp.log1p(jnp.exp(-jnp.abs(x)))


def _split3(x):
    hi = x.astype(BF16)
    r1 = x - hi.astype(F32)
    mid = r1.astype(BF16)
    lo = (r1 - mid.astype(F32)).astype(BF16)
    return hi, mid, lo


def _dot(a, b):
    return jnp.dot(a, b, preferred_element_type=F32)


def _dot_nt(a, b):
    return lax.dot_general(a, b, (((1,), (1,)), ((), ())), preferred_element_type=F32)


def _norm_kernel(x_ref, g_ref, o_ref):
    o_ref[...] = (_rms(x_ref[...]) * g_ref[...]).astype(o_ref.dtype)


def _rmsnorm_bf16(x2d, gain, tm):
    m, d = x2d.shape
    return pl.pallas_call(
        _norm_kernel,
        grid=(m // tm,),
        in_specs=[pl.BlockSpec((tm, d), lambda i: (i, 0)),
                  pl.BlockSpec((1, d), lambda i: (0, 0))],
        out_specs=pl.BlockSpec((tm, d), lambda i: (i, 0)),
        out_shape=jax.ShapeDtypeStruct((m, d), BF16),
        compiler_params=_cparams(("arbitrary",), 32),
        name="rmsnorm",
    )(x2d, gain.reshape(1, d))


def _proj_kernel(h_ref, w_ref, *o_refs, first_scale):
    acc = _dot(h_ref[...], w_ref[...])
    if first_scale != 1.0:
        acc = acc * jnp.where(pl.program_id(0) == 0, first_scale, 1.0)
    for o in o_refs:
        o[...] = acc.astype(o.dtype)


def _project(h2d, w_groups, g0, ng, out_dtypes, tm, first_scale=1.0):
    m, d = h2d.shape
    n = w_groups.shape[-1]
    return pl.pallas_call(
        functools.partial(_proj_kernel, first_scale=first_scale),
        grid=(ng, m // tm),
        in_specs=[pl.BlockSpec((tm, d), lambda g, i: (i, 0)),
                  pl.BlockSpec((None, d, n), lambda g, i: (g + g0, 0, 0))],
        out_specs=[pl.BlockSpec((None, tm, n), lambda g, i: (g, i, 0)) for _ in out_dtypes],
        out_shape=[jax.ShapeDtypeStruct((ng, m, n), dt) for dt in out_dtypes],
        compiler_params=_cparams(("arbitrary", "arbitrary"), 48),
        name="in_proj",
    )(h2d, w_groups)


def _proj_cache_kernel(h_ref, w_ref, *rest):
    o5_ref, o16_ref = rest[-2:]
    acc = _dot(h_ref[...], w_ref[...])
    o16_ref[...] = acc.astype(o16_ref.dtype)
    for h in range(H_A):
        o5_ref[:, h, :] = acc[:, h * DH_A:(h + 1) * DH_A]


def _project_cache(h3, w_groups, g, stacked, layer, depth, tm):
    batch, seq, d = h3.shape
    in_specs = [pl.BlockSpec((None, tm, d), lambda b, i: (b, i, 0)),
                pl.BlockSpec((None, d, d), lambda b, i: (g, 0, 0))]
    args = [h3, w_groups]
    aliases = {}
    if stacked is not None:
        in_specs.append(pl.BlockSpec(memory_space=pl.ANY))
        args.append(stacked)
        aliases = {2: 0}
    return pl.pallas_call(
        _proj_cache_kernel,
        grid=(batch, seq // tm),
        in_specs=in_specs,
        out_specs=[pl.BlockSpec((None, None, tm, H_A, DH_A), lambda b, i: (layer, b, i, 0, 0)),
                   pl.BlockSpec((None, tm, d), lambda b, i: (b, i, 0))],
        out_shape=[jax.ShapeDtypeStruct((depth, batch, seq, H_A, DH_A), F32),
                   jax.ShapeDtypeStruct((batch, seq, d), BF16)],
        input_output_aliases=aliases,
        compiler_params=_cparams(("arbitrary", "arbitrary"), 48),
        name="kv_proj",
    )(*args)


def _fgate_kernel(h_ref, wf_ref, bf_ref, tri_ref, place_ref, ones_ref, logf_ref, qx_ref, kx_ref, carry_ref):
    @pl.when(pl.program_id(1) == 0)
    def _():
        carry_ref[...] = jnp.zeros_like(carry_ref)

    lf = _log_sigmoid(_dot(h_ref[...], wf_ref[...]) + bf_ref[...])
    logf_ref[...] = lf
    tri = tri_ref[...]
    hi, mid, lo = _split3(lf)
    c = (_dot(tri, hi) + _dot(tri, mid)) + _dot(tri, lo) + carry_ref[0:1, :]
    tm = c.shape[0]
    carry_ref[...] = jnp.broadcast_to(c[tm - 1:tm, :], carry_ref.shape)
    lane = lax.broadcasted_iota(jnp.int32, c.shape, 1)
    pieces = [jnp.where(lane < H_A, p.astype(F32), 0.0) for p in _split3(c * LOG2E)]
    x = (pieces[0] + pltpu.roll(pieces[1], H_A, 1) + pltpu.roll(pieces[2], 2 * H_A, 1)).astype(BF16)
    for h in range(H_A):
        ext = _dot(x, place_ref[h]) + ones_ref[...]
        qx_ref[h] = ext[:, :LANES].astype(qx_ref.dtype)
        kx_ref[h] = ext[:, LANES:].astype(kx_ref.dtype)


def _forget_gate(h2d, wf, bf, batch, seq, tm):
    d = h2d.shape[1]
    ns = seq // tm
    tri = jnp.asarray(np.tril(np.ones((tm, tm), np.float32)), BF16)
    place = np.zeros((H_A, LANES, 2 * LANES), np.float32)
    ones = np.zeros((1, 2 * LANES), np.float32)
    for j in range(N_EXT):
        for h in range(H_A):
            place[h, j * H_A + h, j] = 1.0
            place[h, j * H_A + h, LANES + N_EXT + j] = -1.0
        ones[0, N_EXT + j] = 1.0
        ones[0, LANES + j] = 1.0
    const = lambda shape: pl.BlockSpec(shape, lambda b, i: (0,) * len(shape))
    return pl.pallas_call(
        _fgate_kernel,
        grid=(batch, ns),
        in_specs=[pl.BlockSpec((tm, d), lambda b, i: (b * ns + i, 0)),
                  const((d, LANES)), const((1, LANES)), const((tm, tm)),
                  const(place.shape), const(ones.shape)],
        out_specs=[pl.BlockSpec((tm, LANES), lambda b, i: (b * ns + i, 0)),
                   pl.BlockSpec((None, H_A, tm, LANES), lambda b, i: (b, 0, i, 0)),
                   pl.BlockSpec((None, H_A, tm, LANES), lambda b, i: (b, 0, i, 0))],
        out_shape=[jax.ShapeDtypeStruct((batch * seq, LANES), F32),
                   jax.ShapeDtypeStruct((batch, H_A, seq, LANES), BF16),
                   jax.ShapeDtypeStruct((batch, H_A, seq, LANES), BF16)],
        scratch_shapes=[pltpu.VMEM((SUBLANES, LANES), F32)],
        compiler_params=_cparams(("arbitrary", "arbitrary"), 32),
        name="forget_gate",
    )(h2d, wf, bf, tri, jnp.asarray(place, BF16), jnp.asarray(ones))


def _flash_kernel(q_ref, qx_ref, k_ref, kx_ref, v_ref, o_ref, m_ref, acc_ref, s_ref, *, tq, tk, ts):
    qi = pl.program_id(2)
    nsub = tq // ts
    kper = tq // tk
    assert kper == 2, "the two logit buffers are indexed statically: two key blocks per q tile"
    m_ref[...] = jnp.full(m_ref.shape, NEG, F32)
    acc_ref[...] = jnp.zeros_like(acc_ref)
    ones = jnp.ones((tk, DH_A), BF16)

    def logits(kj, buf):
        ks = pl.multiple_of(kj * tk, tk)
        k = jnp.concatenate([k_ref[pl.ds(ks, tk), :], kx_ref[pl.ds(ks, tk), :]], axis=1)
        q = jnp.concatenate([q_ref[...], qx_ref[...]], axis=1)
        s_ref[buf] = _dot_nt(q, k)

    def softmax_pv(kj, buf, chains):
        ks = pl.multiple_of(kj * tk, tk)
        v = jnp.concatenate([v_ref[pl.ds(ks, tk), :], ones], axis=1)
        for r, col_minus_row in chains:
            rows = slice(r * ts, (r + 1) * ts)
            s = s_ref[buf, rows, :]
            if col_minus_row is not None:
                rel = (lax.broadcasted_iota(jnp.int32, (ts, tk), 1)
                       - lax.broadcasted_iota(jnp.int32, (ts, tk), 0))
                s = jnp.where(rel + col_minus_row <= 0, s, NEG)
            m_old = m_ref[rows, :]
            m_new = jnp.maximum(m_old, jnp.max(s, axis=-1, keepdims=True))
            p = jnp.exp2(s - m_new)
            acc_ref[rows, :] = jnp.exp2(m_old - m_new) * acc_ref[rows, :] + _dot(p.astype(BF16), v)
            m_ref[rows, :] = m_new

    all_chains = [(r, None) for r in range(nsub)]

    def pair(it, carry):
        j = it * kper
        logits(j + 1, 1)
        softmax_pv(j, 0, all_chains)
        logits(j + 2, 0)
        softmax_pv(j + 1, 1, all_chains)
        return carry

    logits(0, 0)
    lax.fori_loop(0, qi, pair, 0)
    n_full = qi * kper
    for d in range(kper):
        if d + 1 < kper:
            logits(n_full + d + 1, d + 1)
        chains = []
        for r in range(nsub):
            if d * tk > r * ts + ts - 1:
                continue
            fully_visible = d * tk + tk - 1 <= r * ts
            chains.append((r, None if fully_visible else d * tk - r * ts))
        softmax_pv(n_full + d, d, chains)
    acc = acc_ref[...]
    o_ref[...] = (acc[:, :DH_A] / acc[:, DH_A:]).astype(o_ref.dtype)


def _fox_prompt(q4, qx, k16, kx, v16, tq, tk, ts):
    _, batch, seq, d = q4.shape
    kern = functools.partial(_flash_kernel, tq=tq, tk=tk, ts=ts)
    return pl.pallas_call(
        kern,
        grid=(batch, H_A, seq // tq),
        in_specs=[pl.BlockSpec((None, None, tq, DH_A), lambda b, h, i: (G_Q - N_F32_GROUPS, b, i, h)),
                  pl.BlockSpec((None, None, tq, LANES), lambda b, h, i: (b, h, i, 0)),
                  pl.BlockSpec((None, seq, DH_A), lambda b, h, i: (b, 0, h)),
                  pl.BlockSpec((None, None, seq, LANES), lambda b, h, i: (b, h, 0, 0)),
                  pl.BlockSpec((None, seq, DH_A), lambda b, h, i: (b, 0, h))],
        out_specs=pl.BlockSpec((None, tq, DH_A), lambda b, h, i: (b, i, h)),
        out_shape=jax.ShapeDtypeStruct((batch, seq, d), BF16),
        scratch_shapes=[pltpu.VMEM((tq, 1), F32), pltpu.VMEM((tq, 2 * DH_A), F32),
                        pltpu.VMEM((2, tq, tk), F32)],
        compiler_params=_cparams(("arbitrary", "arbitrary", "arbitrary"), 48),
        name="fox_prompt",
    )(q4, qx, k16, kx, v16)


def _pool_means_minus_u(ext_ref, s_ref, u, inv_cnt, tm):
    gw = u.shape[1] // N_POOL
    n = tm + 15
    s_ref[0:n, :] = ext_ref[1:1 + n, :] + ext_ref[0:n, :]
    outs = [s_ref[15:15 + tm, 0:gw]]
    n = tm + 13
    s_ref[0:n, gw:] = s_ref[2:2 + n, gw:] + s_ref[0:n, gw:]
    outs.append(s_ref[13:13 + tm, gw:2 * gw])
    n = tm + 9
    s_ref[0:n, 2 * gw:] = s_ref[4:4 + n, 2 * gw:] + s_ref[0:n, 2 * gw:]
    outs.append(s_ref[9:9 + tm, 2 * gw:3 * gw])
    n = tm + 1
    s_ref[0:n, 3 * gw:] = s_ref[8:8 + n, 3 * gw:] + s_ref[0:n, 3 * gw:]
    outs.append(s_ref[1:1 + tm, 3 * gw:])
    return [outs[g] * inv_cnt[g] - u[:, g * gw:(g + 1) * gw] for g in range(N_POOL)]


def _gate4(z, g):
    return (1.0 + jnp.tanh(0.5 * z)) * ((1.0 + jnp.tanh(0.5 * g)) * g)


def _merge(oa, ga, ob, gb, om, gm, za, zb, zm):
    return 0.25 * (_gate4(za, ga) * oa + _gate4(zb, gb) * ob + _gate4(zm, gm) * om)


def _pmerge_kernel(oa_ref, ga_ref, gb_ref, qm_ref, gm_ref, za_ref, zb_ref, zm_ref, u_ref, x_ref,
                   km_ref, vm_ref, wpool_ref, pscale_ref, wout_ref, npost_ref, npre_ref,
                   xo_ref, ho_ref, ext_ref, s_ref, *, tm):
    i = pl.program_id(1)

    @pl.when(i == 0)
    def _():
        ext_ref[0:16, :] = jnp.zeros((16, ext_ref.shape[1]), F32)

    u = u_ref[...]
    d = u.shape[1]
    ext_ref[16:16 + tm, :] = u
    pos = i * tm + lax.broadcasted_iota(jnp.int32, (tm, 1), 0)
    inv_cnt = [1.0 / jnp.minimum(w, pos + 1).astype(F32) for w in POOL_WINDOWS]
    dm = _pool_means_minus_u(ext_ref, s_ref, u, inv_cnt, tm)
    ob = jnp.concatenate([_dot(dm[g].astype(BF16), wpool_ref[g]) for g in range(N_POOL)], axis=1)
    ob = ob * pscale_ref[...]
    ext_ref[0:16, :] = ext_ref[tm:tm + 16, :]

    dhm = d // H_M
    oms = []
    for hm in range(H_M):
        sl = slice(hm * dhm, (hm + 1) * dhm)
        s = _dot_nt(qm_ref[:, sl], km_ref[:, sl]) * (dhm ** -0.5 * LOG2E)
        p = jnp.exp2(s - jnp.max(s, axis=-1, keepdims=True))
        den = jnp.sum(p, axis=-1, keepdims=True)
        oms.append(_dot(p.astype(BF16), vm_ref[:, sl]) / den)
    om = jnp.concatenate(oms, axis=1)

    f = lambda r: r[...].astype(F32)
    hmix = _merge(f(oa_ref), f(ga_ref), ob, f(gb_ref), om, f(gm_ref), f(za_ref), f(zb_ref), f(zm_ref))
    y = _dot(hmix.astype(BF16), wout_ref[...])
    xn = x_ref[...] + _rms(y) * npost_ref[...]
    xo_ref[...] = xn
    ho_ref[...] = (_rms(xn) * npre_ref[...]).astype(ho_ref.dtype)


def _prompt_merge(oa, zb8, u32, x3, kvm, wpool, pscale, wout, npost, npre_next, tm):
    batch, seq, d = x3.shape
    nmem = kvm.shape[2]

    def grp(g):
        return pl.BlockSpec((None, None, tm, d), lambda b, i: (g - N_F32_GROUPS, b, i, 0))

    row = pl.BlockSpec((None, tm, d), lambda b, i: (b, i, 0))
    vec = pl.BlockSpec((1, d), lambda b, i: (0, 0))
    kern = functools.partial(_pmerge_kernel, tm=tm)
    return pl.pallas_call(
        kern,
        grid=(batch, seq // tm),
        in_specs=[row, grp(G_GA), grp(G_GB), grp(G_QM), grp(G_GM), grp(G_ZA), grp(G_ZB), grp(G_ZM),
                  row, row,
                  pl.BlockSpec((None, None, nmem, d), lambda b, i: (0, b, 0, 0)),
                  pl.BlockSpec((None, None, nmem, d), lambda b, i: (1, b, 0, 0)),
                  pl.BlockSpec(wpool.shape, lambda b, i: (0, 0, 0)),
                  vec,
                  pl.BlockSpec((d, d), lambda b, i: (0, 0)),
                  vec, vec],
        out_specs=[row, row],
        out_shape=[jax.ShapeDtypeStruct((batch, seq, d), F32),
                   jax.ShapeDtypeStruct((batch, seq, d), BF16)],
        scratch_shapes=[pltpu.VMEM((tm + 16, d), F32), pltpu.VMEM((tm + 16, d), F32)],
        compiler_params=_cparams(("arbitrary", "arbitrary"), 56),
        name="prompt_merge",
    )(oa, zb8, zb8, zb8, zb8, zb8, zb8, zb8, u32, x3, kvm, kvm, wpool, pscale, wout, npost, npre_next)


def _dec_fgate_kernel(h_ref, wf_ref, bf_ref, o_ref):
    o_ref[...] = _log_sigmoid(_dot(h_ref[...], wf_ref[...]) + bf_ref[...])


def _dec_forget_gate(h2d, wf, bf):
    m = h2d.shape[0]
    return pl.pallas_call(
        _dec_fgate_kernel,
        out_shape=jax.ShapeDtypeStruct((m, LANES), F32),
        name="dec_forget_gate",
    )(h2d, wf, bf)


def _dec_bias_kernel(pt_ref, lfnew_ref, sufm_ref, pgm_ref, *rest, npages):
    page_refs = rest[:npages]
    o_ref = rest[npages]
    x = jnp.concatenate([r[...] for r in page_refs], axis=0)
    hi, mid, lo = _split3(x)
    sufm = sufm_ref[...]
    within = (_dot(hi, sufm) + _dot(mid, sufm)) + _dot(lo, sufm)
    ones = jnp.ones((LANES, LANES), BF16)
    tot = (_dot(hi, ones) + _dot(mid, ones)) + _dot(lo, ones)
    th, tmid, tl = _split3(tot)
    pgm = pgm_ref[...]
    later = (_dot(pgm, th) + _dot(pgm, tmid)) + _dot(pgm, tl)
    lfnew = lfnew_ref[...]
    bias = (within + later).reshape(npages, H_A, LANES) + lfnew[None]
    bias = bias.reshape(npages * H_A, LANES) * LOG2E

    bh = bias.astype(BF16).astype(F32)
    bl = (bias - bh).astype(BF16).astype(F32)
    half = LANES // 2
    lane = lax.broadcasted_iota(jnp.int32, bias.shape, 1)
    first = jnp.where(lane < half, bh, pltpu.roll(bl, half, 1))
    second = jnp.where(lane < half, pltpu.roll(bh, half, 1), bl)
    first = first.reshape(npages, 1, H_A, LANES)
    second = second.reshape(npages, 1, H_A, LANES)
    halves = jnp.concatenate([first, second], axis=1)
    o_ref[...] = jnp.concatenate([halves, halves], axis=2).astype(o_ref.dtype)


def _dec_bias(page_table, lfnew_rep, logf_t, layer):
    batch, npages = page_table.shape
    page = logf_t.shape[-1]
    assert page == LANES
    sufm = jnp.asarray(np.tril(np.ones((page, page), np.float32), -1), BF16)
    pg = np.arange(npages * H_A)
    pgm_np = ((pg[None, :] // H_A > pg[:, None] // H_A) & (pg[None, :] % H_A == pg[:, None] % H_A))
    pgm = jnp.asarray(pgm_np.astype(np.float32), BF16)

    def page_spec(j):
        return pl.BlockSpec((None, None, H_A, LANES), lambda b, pt: (layer, pt[b, j], 0, 0))

    kern = functools.partial(_dec_bias_kernel, npages=npages)
    grid_spec = pltpu.PrefetchScalarGridSpec(
        num_scalar_prefetch=1,
        grid=(batch,),
        in_specs=[pl.BlockSpec((None, H_A, LANES), lambda b, pt: (b, 0, 0)),
                  pl.BlockSpec((page, page), lambda b, pt: (0, 0)),
                  pl.BlockSpec(pgm.shape, lambda b, pt: (0, 0))]
                 + [page_spec(j) for j in range(npages)],
        out_specs=pl.BlockSpec((None, npages, 2, 2 * H_A, LANES), lambda b, pt: (b, 0, 0, 0, 0)),
    )
    return pl.pallas_call(
        kern,
        grid_spec=grid_spec,
        out_shape=jax.ShapeDtypeStruct((batch, npages, 2, 2 * H_A, LANES), BF16),
        compiler_params=_cparams(("arbitrary",), 32),
        name="dec_bias",
    )(page_table, lfnew_rep, sufm, pgm, *([logf_t] * npages))


def _dec_attn_kernel(pt_ref, q_ref, kn_ref, vn_ref, bias_ref, mask_ref, *rest, ppb, scale):
    k_refs = rest[:ppb]
    v_refs = rest[ppb:2 * ppb]
    o_ref = rest[2 * ppb]
    m_ref, l_ref, acc_ref = rest[2 * ppb + 1:]
    j = pl.program_id(1)

    @pl.when(j == 0)
    def _():
        m_ref[...] = jnp.full(m_ref.shape, NEG, F32)
        l_ref[...] = jnp.zeros_like(l_ref)
        acc_ref[...] = jnp.zeros_like(acc_ref)

    qs = q_ref[...] * (scale * LOG2E)
    ones = jnp.ones((2 * LANES, LANES), BF16)
    page = k_refs[0].shape[0]
    half = page // 2
    grp = 16
    ms, ls, accs = [], [], []
    for r in range(ppb):
        k = k_refs[r][...]
        prod = (k * qs[None]).reshape(page * H_A, DH_A).astype(BF16)
        lbs = []
        for hh in range(2):
            t2 = bias_ref[r, hh]
            lbs.append((t2[None] * mask_ref[...]).reshape(half * H_A, LANES))
        lhs = jnp.concatenate([prod, jnp.concatenate(lbs, axis=0)], axis=1)
        s = _dot(lhs, ones).reshape(page, H_A, LANES)
        for g in range(page // grp):
            sg = s[g * grp:(g + 1) * grp]
            m_g = jnp.max(sg, axis=0)
            p = jnp.exp2(sg - m_g[None])
            ms.append(m_g)
            ls.append(jnp.sum(p, axis=0))
            accs.append(jnp.sum(p * v_refs[r][g * grp:(g + 1) * grp], axis=0))
    m_old = m_ref[...]
    m = functools.reduce(jnp.maximum, ms, m_old)
    a_old = jnp.exp2(m_old - m)
    l = a_old * l_ref[...]
    acc = a_old * acc_ref[...]
    for m_g, l_g, acc_g in zip(ms, ls, accs):
        w = jnp.exp2(m_g - m)
        l = l + w * l_g
        acc = acc + w * acc_g
    m_ref[...], l_ref[...], acc_ref[...] = m, l, acc

    @pl.when(j == pl.num_programs(1) - 1)
    def _():
        s_new = jnp.sum(qs * kn_ref[...], axis=-1, keepdims=True)
        m_fin = jnp.maximum(m, s_new)
        p_new = jnp.exp2(s_new - m_fin)
        a = jnp.exp2(m - m_fin)
        o_ref[...] = (a * acc + p_new * vn_ref[...]) / (a * l + p_new)


def _dec_attn(page_table, q3, kn3, vn3, bias5, cache_k, cache_v, layer, ppb):
    batch, npages = page_table.shape
    page = cache_k.shape[2]
    half = page // 2
    mask_np = np.zeros((half // 2, 2 * H_A, LANES), np.float32)
    for i in range(half // 2):
        for s in range(2):
            pos = 2 * i + s
            mask_np[i, s * H_A:(s + 1) * H_A, pos] = 1.0
            mask_np[i, s * H_A:(s + 1) * H_A, half + pos] = 1.0
    mask = jnp.asarray(mask_np, BF16)
    vec = pl.BlockSpec((None, H_A, DH_A), lambda b, j, pt: (b, 0, 0))

    def page_spec(r):
        return pl.BlockSpec((None, None, page, H_A, DH_A),
                            lambda b, j, pt: (layer, pt[b, j * ppb + r], 0, 0, 0))

    kern = functools.partial(_dec_attn_kernel, ppb=ppb, scale=DH_A ** -0.5)
    grid_spec = pltpu.PrefetchScalarGridSpec(
        num_scalar_prefetch=1,
        grid=(batch, npages // ppb),
        in_specs=[vec, vec, vec,
                  pl.BlockSpec((None, ppb, 2, 2 * H_A, LANES), lambda b, j, pt: (b, j, 0, 0, 0)),
                  pl.BlockSpec(mask.shape, lambda b, j, pt: (0, 0, 0))]
                 + [page_spec(r) for r in range(ppb)] * 2,
        out_specs=vec,
        scratch_shapes=[pltpu.VMEM((H_A, DH_A), F32)] * 3,
    )
    return pl.pallas_call(
        kern,
        grid_spec=grid_spec,
        out_shape=jax.ShapeDtypeStruct((batch, H_A, DH_A), F32),
        compiler_params=_cparams(("arbitrary", "arbitrary"), 48),
        name="dec_attn",
    )(page_table, q3, kn3, vn3, bias5, mask, *([cache_k] * ppb), *([cache_v] * ppb))


def _dec_mem_kernel(q_ref, k_ref, v_ref, o_ref, *, scale):
    q = q_ref[...] * (scale * LOG2E)
    s = jnp.sum(k_ref[...] * q[None], axis=-1, keepdims=True)
    p = jnp.exp2(s - jnp.max(s, axis=0, keepdims=True))
    num = jnp.sum(p * v_ref[...], axis=0)
    o_ref[...] = num / jnp.sum(p, axis=0)


def _dec_mem_attn(qm, cache_km, cache_vm, layer):
    batch, hm, dhm = qm.shape
    nmem = cache_km.shape[2]
    kern = functools.partial(_dec_mem_kernel, scale=dhm ** -0.5)
    vec = pl.BlockSpec((None, hm, dhm), lambda b: (b, 0, 0))
    mem = pl.BlockSpec((None, None, nmem, hm, dhm), lambda b: (layer, b, 0, 0, 0))
    return pl.pallas_call(
        kern,
        grid=(batch,),
        in_specs=[vec, mem, mem],
        out_specs=vec,
        out_shape=jax.ShapeDtypeStruct((batch, hm, dhm), F32),
        compiler_params=_cparams(("arbitrary",), 32),
        name="dec_mem_attn",
    )(qm, cache_km, cache_vm)


def _dec_tail_kernel(z_ref, oa_ref, om_ref, state_ref, x_ref, wpool_ref, pscale_ref, wout_ref,
                     npost_ref, npre_ref, xo_ref, ho_ref, *, past_len):
    z = lambda g: z_ref[g]
    u = z(G_U)
    d = u.shape[1]
    gw = d // N_POOL
    dms = []
    for g, w in enumerate(POOL_WINDOWS):
        sl = slice(g * gw, (g + 1) * gw)
        tot = u[:, sl]
        for t in range(1, w):
            tot = tot + state_ref[POOL_STATE - t, :, sl]
        dms.append(tot * (1.0 / min(w, past_len + 1)) - u[:, sl])
    ob = jnp.concatenate([_dot(dms[g].astype(BF16), wpool_ref[g]) for g in range(N_POOL)], axis=1)
    ob = ob * pscale_ref[...]
    hmix = _merge(oa_ref[...], z(G_GA), ob, z(G_GB), om_ref[...], z(G_GM), z(G_ZA), z(G_ZB), z(G_ZM))
    y = _dot(hmix.astype(BF16), wout_ref[...])
    xn = x_ref[...] + _rms(y) * npost_ref[...]
    xo_ref[...] = xn
    ho_ref[...] = (_rms(xn) * npre_ref[...]).astype(ho_ref.dtype)


def _dec_tail(z, oa, om, state_t, x2, wpool, pscale, wout, npost, npre_next, past_len):
    b, d = x2.shape
    return pl.pallas_call(
        functools.partial(_dec_tail_kernel, past_len=past_len),
        out_shape=[jax.ShapeDtypeStruct((b, d), F32),
                   jax.ShapeDtypeStruct((b, d), BF16)],
        compiler_params=pltpu.CompilerParams(vmem_limit_bytes=48 * 1024 * 1024),
        name="dec_tail",
    )(z, oa, om, state_t, x2, wpool, pscale, wout, npost, npre_next)


def _stack_in_proj(w_in):
    d = w_in.shape[1]
    o = 0
    parts = {}
    for name, width in (("q", d), ("k", d), ("v", d), ("f", H_A), ("ga", d), ("u", d), ("gb", d),
                        ("qm", d), ("gm", d), ("za", d), ("zb", d), ("zm", d)):
        parts[name] = w_in[:, :, o:o + width]
        o += width
    order = ("k", "v", "u", "q", "ga", "gb", "qm", "gm", "za", "zb", "zm")
    wg = jnp.stack([parts[n] for n in order], axis=1).astype(BF16)
    wf = jnp.pad(parts["f"], ((0, 0), (0, 0), (0, LANES - H_A))).astype(BF16)
    return wg, wf


def kernel(x_prompt, x_sample, mem_prompt, cache_k_attn, cache_v_attn, cache_logf_attn, cache_k_mem,
           cache_v_mem, state_pool, page_table, norm_pre, w_in, b_forget, w_pool, pool_scale, norm_mem,
           w_mem_kv, w_out, norm_post):
    bp, sp, d = x_prompt.shape
    bd, sd, _ = x_sample.shape
    assert sd == 1 and d == H_A * DH_A
    depth = w_in.shape[0]
    nmem = mem_prompt.shape[1]
    mp = bp * sp

    wg, wf = _stack_in_proj(w_in)
    bfp = jnp.pad(b_forget, ((0, 0), (0, LANES - H_A)))[:, None, :]
    wpool = w_pool.astype(BF16)
    wout = w_out.astype(BF16)
    wmem = jnp.stack(jnp.split(w_mem_kv, 2, axis=-1), axis=1).astype(BF16)
    npre = jnp.concatenate([norm_pre, norm_pre[:1]], axis=0)[:, None, :]
    npost = norm_post[:, None, :]
    pscale = pool_scale[:, None, :]
    logf_t = jnp.swapaxes(cache_logf_attn, 2, 3)
    state_t = jnp.swapaxes(state_pool, 1, 2)
    past_len = page_table.shape[1] * cache_k_attn.shape[2]

    tm_proj = min(1024, sp)
    tm_merge = min(256, sp)
    tq = min(1024, sp)
    tk = min(512, tq)
    ts = min(256, tk)
    ppb = 8 if page_table.shape[1] % 8 == 0 else 1
    q_scale = DH_A ** -0.5 * LOG2E

    xp = x_prompt
    xs = x_sample.reshape(bd, d)
    hp = _rmsnorm_bf16(xp.reshape(mp, d), norm_pre[0], min(512, mp))
    hs = _rmsnorm_bf16(xs, norm_pre[0], bd)
    mem2 = mem_prompt.reshape(bp * nmem, d)

    k_all = v_all = None
    outs = {k: [] for k in ("fp", "kmp", "vmp", "pp", "kd", "vd", "fd", "pd")}
    for l in range(depth):
        hp3 = hp.reshape(bp, sp, d)
        k_all, k16 = _project_cache(hp3, wg[l], G_K, k_all, l, depth, tm_proj)
        v_all, v16 = _project_cache(hp3, wg[l], G_V, v_all, l, depth, tm_proj)
        (u32,) = _project(hp, wg[l], G_U, 1, (F32,), tm_proj)
        (zb8,) = _project(hp, wg[l], N_F32_GROUPS, 11 - N_F32_GROUPS, (BF16,), tm_proj, first_scale=q_scale)
        logf, qx, kx = _forget_gate(hp, wf[l], bfp[l], bp, sp, min(512, sp))
        u32 = u32.reshape(bp, sp, d)
        zb8 = zb8.reshape(11 - N_F32_GROUPS, bp, sp, d)
        oa = _fox_prompt(zb8, qx, k16, kx, v16, tq, tk, ts)

        hm = _rmsnorm_bf16(mem2, norm_mem[l], min(512, bp * nmem))
        kvm32, kvm16 = _project(hm, wmem[l], 0, 2, (F32, BF16), min(512, bp * nmem))
        xp, hp3 = _prompt_merge(oa, zb8, u32, xp, kvm16.reshape(2, bp, nmem, d), wpool[l], pscale[l],
                                wout[l], npost[l], npre[l + 1], tm_merge)
        hp = hp3.reshape(mp, d)
        outs["fp"].append(logf.reshape(bp, sp, LANES)[:, :, :H_A])
        outs["kmp"].append(kvm32[0].reshape(bp, nmem, H_M, d // H_M))
        outs["vmp"].append(kvm32[1].reshape(bp, nmem, H_M, d // H_M))
        outs["pp"].append(u32[:, sp - POOL_STATE:])

        (zs,) = _project(hs, wg[l], 0, 11, (F32,), bd)
        lfnew = _dec_forget_gate(hs, wf[l], bfp[l])
        lfnew_rep = jnp.broadcast_to(lfnew[:, :H_A, None], (bd, H_A, LANES))
        bias5 = _dec_bias(page_table, lfnew_rep, logf_t, l)
        q3 = zs[G_Q].reshape(bd, H_A, DH_A)
        kn3 = zs[G_K].reshape(bd, H_A, DH_A)
        vn3 = zs[G_V].reshape(bd, H_A, DH_A)
        oa_s = _dec_attn(page_table, q3, kn3, vn3, bias5, cache_k_attn, cache_v_attn, l, ppb)
        om_s = _dec_mem_attn(zs[G_QM].reshape(bd, H_M, d // H_M), cache_k_mem, cache_v_mem, l)
        xs, hs = _dec_tail(zs, oa_s.reshape(bd, d), om_s.reshape(bd, d), state_t[l], xs,
                           wpool[l], pscale[l], wout[l], npost[l], npre[l + 1], past_len)
        st_new = jnp.concatenate([state_pool[l][:, 1:], zs[G_U][:, None, :]], axis=1)
        outs["kd"].append(kn3.reshape(bd, 1, H_A, DH_A))
        outs["vd"].append(vn3.reshape(bd, 1, H_A, DH_A))
        outs["fd"].append(lfnew[:, None, :H_A])
        outs["pd"].append(st_new)

    st = {k: jnp.stack(v) for k, v in outs.items()}
    return (xp, xs.reshape(bd, 1, d), k_all, v_all, st["fp"], st["kmp"], st["vmp"], st["pp"],
            st["kd"], st["vd"], st["fd"], st["pd"])
```

```python
import functools

import jax
import jax.numpy as jnp
import numpy as np
from jax import lax
from jax.experimental import pallas as pl
from jax.experimental.pallas import tpu as pltpu

F32 = jnp.float32
BF16 = jnp.bfloat16

RMS_EPS = 1e-6
H_A = 8
DH_A = 128
N_POOL = 4
POOL_WINDOWS = (2, 4, 8, 16)
POOL_STATE = 15
H_M = 4
NEG = -1e30
LOG2E = 1.4426950408889634

LANES = 128
SUBLANES = 8

C_Q, C_K, C_V, C_GA, C_U, C_GB, C_QM, C_GM, C_ZA, C_ZB, C_ZM = range(11)
G_K, G_V, G_U, G_Q, G_GA, G_GB, G_QM, G_GM, G_ZA, G_ZB, G_ZM = range(11)
N_F32_GROUPS = 3
N_EXT = 3


def _cparams(sem, vmem_mb):
    return pltpu.CompilerParams(dimension_semantics=sem, vmem_limit_bytes=vmem_mb * 1024 * 1024)


def _rms(x):
    return x * lax.rsqrt(jnp.mean(x * x, axis=-1, keepdims=True) + RMS_EPS)


def _log_sigmoid(x):
    return jnp.minimum(x, 0.0) - jnp.log1p(jnp.exp(-jnp.abs(x)))


def _split3(x):
    hi = x.astype(BF16)
    r1 = x - hi.astype(F32)
    mid = r1.astype(BF16)
    lo = (r1 - mid.astype(F32)).astype(BF16)
    return hi, mid, lo


def _dot(a, b):
    return jnp.dot(a, b, preferred_element_type=F32)


def _dot_nt(a, b):
    return lax.dot_general(a, b, (((1,), (1,)), ((), ())), preferred_element_type=F32)


def _norm_kernel(x_ref, g_ref, o_ref):
    o_ref[...] = (_rms(x_ref[...]) * g_ref[...]).astype(o_ref.dtype)


def _rmsnorm_bf16(x2d, gain, tm):
    m, d = x2d.shape
    return pl.pallas_call(
        _norm_kernel,
        grid=(m // tm,),
        in_specs=[pl.BlockSpec((tm, d), lambda i: (i, 0)),
                  pl.BlockSpec((1, d), lambda i: (0, 0))],
        out_specs=pl.BlockSpec((tm, d), lambda i: (i, 0)),
        out_shape=jax.ShapeDtypeStruct((m, d), BF16),
        compiler_params=_cparams(("arbitrary",), 32),
        name="rmsnorm",
    )(x2d, gain.reshape(1, d))


def _proj_kernel(h_ref, w_ref, *o_refs, first_scale):
    acc = _dot(h_ref[...], w_ref[...])
    if first_scale != 1.0:
        acc = acc * jnp.where(pl.program_id(0) == 0, first_scale, 1.0)
    for o in o_refs:
        o[...] = acc.astype(o.dtype)


def _project(h2d, w, layer, col_of_group, ng, out_dtypes, tm, first_scale=1.0):
    m, d = h2d.shape
    n = d
    return pl.pallas_call(
        functools.partial(_proj_kernel, first_scale=first_scale),
        grid=(ng, m // tm),
        in_specs=[pl.BlockSpec((tm, d), lambda g, i: (i, 0)),
                  pl.BlockSpec((None, d, n), lambda g, i: (layer, 0, col_of_group(g)))],
        out_specs=[pl.BlockSpec((None, tm, n), lambda g, i: (g, i, 0)) for _ in out_dtypes],
        out_shape=[jax.ShapeDtypeStruct((ng, m, n), dt) for dt in out_dtypes],
        compiler_params=_cparams(("arbitrary", "arbitrary"), 48),
        name="in_proj",
    )(h2d, w)


def _proj_cache_kernel(h_ref, w_ref, *rest):
    o5_ref, o16_ref = rest[-2:]
    acc = _dot(h_ref[...], w_ref[...])
    o16_ref[...] = acc.astype(o16_ref.dtype)
    for h in range(H_A):
        o5_ref[:, h, :] = acc[:, h * DH_A:(h + 1) * DH_A]


def _project_cache(h3, w, col, stacked, layer, depth, tm):
    batch, seq, d = h3.shape
    in_specs = [pl.BlockSpec((None, tm, d), lambda b, i: (b, i, 0)),
                pl.BlockSpec((None, d, d), lambda b, i: (layer, 0, col))]
    args = [h3, w]
    aliases = {}
    if stacked is not None:
        in_specs.append(pl.BlockSpec(memory_space=pl.ANY))
        args.append(stacked)
        aliases = {2: 0}
    return pl.pallas_call(
        _proj_cache_kernel,
        grid=(batch, seq // tm),
        in_specs=in_specs,
        out_specs=[pl.BlockSpec((None, None, tm, H_A, DH_A), lambda b, i: (layer, b, i, 0, 0)),
                   pl.BlockSpec((None, tm, d), lambda b, i: (b, i, 0))],
        out_shape=[jax.ShapeDtypeStruct((depth, batch, seq, H_A, DH_A), F32),
                   jax.ShapeDtypeStruct((batch, seq, d), BF16)],
        input_output_aliases=aliases,
        compiler_params=_cparams(("arbitrary", "arbitrary"), 48),
        name="kv_proj",
    )(*args)


def _fgate_kernel(h_ref, wf_ref, bf_ref, tri_ref, place_ref, ones_ref, logf_ref, qx_ref, kx_ref, carry_ref):
    @pl.when(pl.program_id(1) == 0)
    def _():
        carry_ref[...] = jnp.zeros_like(carry_ref)

    lf = _log_sigmoid(_dot(h_ref[...], wf_ref[...]) + bf_ref[...])
    logf_ref[...] = lf
    tri = tri_ref[...]
    hi, mid, lo = _split3(lf)
    c = (_dot(tri, hi) + _dot(tri, mid)) + _dot(tri, lo) + carry_ref[0:1, :]
    tm = c.shape[0]
    carry_ref[...] = jnp.broadcast_to(c[tm - 1:tm, :], carry_ref.shape)
    lane = lax.broadcasted_iota(jnp.int32, c.shape, 1)
    pieces = [jnp.where(lane < H_A, p.astype(F32), 0.0) for p in _split3(c * LOG2E)]
    x = (pieces[0] + pltpu.roll(pieces[1], H_A, 1) + pltpu.roll(pieces[2], 2 * H_A, 1)).astype(BF16)
    for h in range(H_A):
        ext = _dot(x, place_ref[h]) + ones_ref[...]
        qx_ref[h] = ext[:, :LANES].astype(qx_ref.dtype)
        kx_ref[h] = ext[:, LANES:].astype(kx_ref.dtype)


def _forget_gate(h2d, wf, bf, batch, seq, tm):
    d = h2d.shape[1]
    ns = seq // tm
    tri = jnp.asarray(np.tril(np.ones((tm, tm), np.float32)), BF16)
    place = np.zeros((H_A, LANES, 2 * LANES), np.float32)
    ones = np.zeros((1, 2 * LANES), np.float32)
    for j in range(N_EXT):
        for h in range(H_A):
            place[h, j * H_A + h, j] = 1.0
            place[h, j * H_A + h, LANES + N_EXT + j] = -1.0
        ones[0, N_EXT + j] = 1.0
        ones[0, LANES + j] = 1.0
    const = lambda shape: pl.BlockSpec(shape, lambda b, i: (0,) * len(shape))
    return pl.pallas_call(
        _fgate_kernel,
        grid=(batch, ns),
        in_specs=[pl.BlockSpec((tm, d), lambda b, i: (b * ns + i, 0)),
                  const((d, LANES)), const((1, LANES)), const((tm, tm)),
                  const(place.shape), const(ones.shape)],
        out_specs=[pl.BlockSpec((tm, LANES), lambda b, i: (b * ns + i, 0)),
                   pl.BlockSpec((None, H_A, tm, LANES), lambda b, i: (b, 0, i, 0)),
                   pl.BlockSpec((None, H_A, tm, LANES), lambda b, i: (b, 0, i, 0))],
        out_shape=[jax.ShapeDtypeStruct((batch * seq, LANES), F32),
                   jax.ShapeDtypeStruct((batch, H_A, seq, LANES), BF16),
                   jax.ShapeDtypeStruct((batch, H_A, seq, LANES), BF16)],
        scratch_shapes=[pltpu.VMEM((SUBLANES, LANES), F32)],
        compiler_params=_cparams(("arbitrary", "arbitrary"), 32),
        name="forget_gate",
    )(h2d, wf, bf, tri, jnp.asarray(place, BF16), jnp.asarray(ones))


def _flash_kernel(q_ref, qx_ref, k_ref, kx_ref, v_ref, o_ref, m_ref, acc_ref, s_ref, *, tq, tk, ts):
    qi = pl.program_id(2)
    nsub = tq // ts
    kper = tq // tk
    assert kper == 2, "the two logit buffers are indexed statically: two key blocks per q tile"
    m_ref[...] = jnp.full(m_ref.shape, NEG, F32)
    acc_ref[...] = jnp.zeros_like(acc_ref)
    ones = jnp.ones((tk, DH_A), BF16)

    def logits(kj, buf):
        ks = pl.multiple_of(kj * tk, tk)
        k = jnp.concatenate([k_ref[pl.ds(ks, tk), :], kx_ref[pl.ds(ks, tk), :]], axis=1)
        q = jnp.concatenate([q_ref[...], qx_ref[...]], axis=1)
        s_ref[buf] = _dot_nt(q, k)

    def softmax_pv(kj, buf, chains):
        ks = pl.multiple_of(kj * tk, tk)
        v = jnp.concatenate([v_ref[pl.ds(ks, tk), :], ones], axis=1)
        for r, col_minus_row in chains:
            rows = slice(r * ts, (r + 1) * ts)
            s = s_ref[buf, rows, :]
            if col_minus_row is not None:
                rel = (lax.broadcasted_iota(jnp.int32, (ts, tk), 1)
                       - lax.broadcasted_iota(jnp.int32, (ts, tk), 0))
                s = jnp.where(rel + col_minus_row <= 0, s, NEG)
            m_old = m_ref[rows, :]
            m_new = jnp.maximum(m_old, jnp.max(s, axis=-1, keepdims=True))
            p = jnp.exp2(s - m_new)
            acc_ref[rows, :] = jnp.exp2(m_old - m_new) * acc_ref[rows, :] + _dot(p.astype(BF16), v)
            m_ref[rows, :] = m_new

    all_chains = [(r, None) for r in range(nsub)]

    def pair(it, carry):
        j = it * kper
        logits(j + 1, 1)
        softmax_pv(j, 0, all_chains)
        logits(j + 2, 0)
        softmax_pv(j + 1, 1, all_chains)
        return carry

    logits(0, 0)
    lax.fori_loop(0, qi, pair, 0)
    n_full = qi * kper
    for d in range(kper):
        if d + 1 < kper:
            logits(n_full + d + 1, d + 1)
        chains = []
        for r in range(nsub):
            if d * tk > r * ts + ts - 1:
                continue
            fully_visible = d * tk + tk - 1 <= r * ts
            chains.append((r, None if fully_visible else d * tk - r * ts))
        softmax_pv(n_full + d, d, chains)
    acc = acc_ref[...]
    o_ref[...] = (acc[:, :DH_A] / acc[:, DH_A:]).astype(o_ref.dtype)


def _fox_prompt(q4, qx, k16, kx, v16, tq, tk, ts):
    _, batch, seq, d = q4.shape
    kern = functools.partial(_flash_kernel, tq=tq, tk=tk, ts=ts)
    return pl.pallas_call(
        kern,
        grid=(batch, H_A, seq // tq),
        in_specs=[pl.BlockSpec((None, None, tq, DH_A), lambda b, h, i: (G_Q - N_F32_GROUPS, b, i, h)),
                  pl.BlockSpec((None, None, tq, LANES), lambda b, h, i: (b, h, i, 0)),
                  pl.BlockSpec((None, seq, DH_A), lambda b, h, i: (b, 0, h)),
                  pl.BlockSpec((None, None, seq, LANES), lambda b, h, i: (b, h, 0, 0)),
                  pl.BlockSpec((None, seq, DH_A), lambda b, h, i: (b, 0, h))],
        out_specs=pl.BlockSpec((None, tq, DH_A), lambda b, h, i: (b, i, h)),
        out_shape=jax.ShapeDtypeStruct((batch, seq, d), BF16),
        scratch_shapes=[pltpu.VMEM((tq, 1), F32), pltpu.VMEM((tq, 2 * DH_A), F32),
                        pltpu.VMEM((2, tq, tk), F32)],
        compiler_params=_cparams(("arbitrary", "arbitrary", "arbitrary"), 48),
        name="fox_prompt",
    )(q4, qx, k16, kx, v16)


def _pool_means_minus_u(ext_ref, s_ref, u, inv_cnt, tm):
    gw = u.shape[1] // N_POOL
    n = tm + 15
    s_ref[0:n, :] = ext_ref[1:1 + n, :] + ext_ref[0:n, :]
    outs = [s_ref[15:15 + tm, 0:gw]]
    n = tm + 13
    s_ref[0:n, gw:] = s_ref[2:2 + n, gw:] + s_ref[0:n, gw:]
    outs.append(s_ref[13:13 + tm, gw:2 * gw])
    n = tm + 9
    s_ref[0:n, 2 * gw:] = s_ref[4:4 + n, 2 * gw:] + s_ref[0:n, 2 * gw:]
    outs.append(s_ref[9:9 + tm, 2 * gw:3 * gw])
    n = tm + 1
    s_ref[0:n, 3 * gw:] = s_ref[8:8 + n, 3 * gw:] + s_ref[0:n, 3 * gw:]
    outs.append(s_ref[1:1 + tm, 3 * gw:])
    return [outs[g] * inv_cnt[g] - u[:, g * gw:(g + 1) * gw] for g in range(N_POOL)]


def _gate4(z, g):
    return (1.0 + jnp.tanh(0.5 * z)) * ((1.0 + jnp.tanh(0.5 * g)) * g)


def _merge(oa, ga, ob, gb, om, gm, za, zb, zm):
    return 0.25 * (_gate4(za, ga) * oa + _gate4(zb, gb) * ob + _gate4(zm, gm) * om)


def _pmerge_kernel(oa_ref, ga_ref, gb_ref, qm_ref, gm_ref, za_ref, zb_ref, zm_ref, u_ref, x_ref,
                   km_ref, vm_ref, wpool_ref, pscale_ref, wout_ref, npost_ref, npre_ref,
                   xo_ref, ho_ref, ext_ref, s_ref, *, tm):
    i = pl.program_id(1)

    @pl.when(i == 0)
    def _():
        ext_ref[0:16, :] = jnp.zeros((16, ext_ref.shape[1]), F32)

    u = u_ref[...]
    d = u.shape[1]
    ext_ref[16:16 + tm, :] = u
    pos = i * tm + lax.broadcasted_iota(jnp.int32, (tm, 1), 0)
    inv_cnt = [1.0 / jnp.minimum(w, pos + 1).astype(F32) for w in POOL_WINDOWS]
    dm = _pool_means_minus_u(ext_ref, s_ref, u, inv_cnt, tm)
    ob = jnp.concatenate([_dot(dm[g].astype(BF16), wpool_ref[g]) for g in range(N_POOL)], axis=1)
    ob = ob * pscale_ref[...]
    ext_ref[0:16, :] = ext_ref[tm:tm + 16, :]

    dhm = d // H_M
    oms = []
    for hm in range(H_M):
        sl = slice(hm * dhm, (hm + 1) * dhm)
        s = _dot_nt(qm_ref[:, sl], km_ref[:, sl]) * (dhm ** -0.5 * LOG2E)
        p = jnp.exp2(s - jnp.max(s, axis=-1, keepdims=True))
        den = jnp.sum(p, axis=-1, keepdims=True)
        oms.append(_dot(p.astype(BF16), vm_ref[:, sl]) / den)
    om = jnp.concatenate(oms, axis=1)

    f = lambda r: r[...].astype(F32)
    hmix = _merge(f(oa_ref), f(ga_ref), ob, f(gb_ref), om, f(gm_ref), f(za_ref), f(zb_ref), f(zm_ref))
    y = _dot(hmix.astype(BF16), wout_ref[...])
    xn = x_ref[...] + _rms(y) * npost_ref[...]
    xo_ref[...] = xn
    ho_ref[...] = (_rms(xn) * npre_ref[...]).astype(ho_ref.dtype)


def _prompt_merge(oa, zb8, u32, x3, kvm, wpool, pscale, wout, npost, npre_next, tm):
    batch, seq, d = x3.shape
    nmem = kvm.shape[2]

    def grp(g):
        return pl.BlockSpec((None, None, tm, d), lambda b, i: (g - N_F32_GROUPS, b, i, 0))

    row = pl.BlockSpec((None, tm, d), lambda b, i: (b, i, 0))
    vec = pl.BlockSpec((1, d), lambda b, i: (0, 0))
    kern = functools.partial(_pmerge_kernel, tm=tm)
    return pl.pallas_call(
        kern,
        grid=(batch, seq // tm),
        in_specs=[row, grp(G_GA), grp(G_GB), grp(G_QM), grp(G_GM), grp(G_ZA), grp(G_ZB), grp(G_ZM),
                  row, row,
                  pl.BlockSpec((None, None, nmem, d), lambda b, i: (0, b, 0, 0)),
                  pl.BlockSpec((None, None, nmem, d), lambda b, i: (1, b, 0, 0)),
                  pl.BlockSpec(wpool.shape, lambda b, i: (0, 0, 0)),
                  vec,
                  pl.BlockSpec((d, d), lambda b, i: (0, 0)),
                  vec, vec],
        out_specs=[row, row],
        out_shape=[jax.ShapeDtypeStruct((batch, seq, d), F32),
                   jax.ShapeDtypeStruct((batch, seq, d), BF16)],
        scratch_shapes=[pltpu.VMEM((tm + 16, d), F32), pltpu.VMEM((tm + 16, d), F32)],
        compiler_params=_cparams(("arbitrary", "arbitrary"), 56),
        name="prompt_merge",
    )(oa, zb8, zb8, zb8, zb8, zb8, zb8, zb8, u32, x3, kvm, kvm, wpool, pscale, wout, npost, npre_next)


def _dec_fgate_kernel(h_ref, wf_ref, bf_ref, o_ref):
    o_ref[...] = _log_sigmoid(_dot(h_ref[...], wf_ref[...]) + bf_ref[...])


def _dec_forget_gate(h2d, wf, bf):
    m = h2d.shape[0]
    return pl.pallas_call(
        _dec_fgate_kernel,
        out_shape=jax.ShapeDtypeStruct((m, LANES), F32),
        name="dec_forget_gate",
    )(h2d, wf, bf)


def _dec_bias_kernel(pt_ref, lfnew_ref, sufm_ref, pgm_ref, *rest, npages):
    page_refs = rest[:npages]
    o_ref = rest[npages]
    x = jnp.concatenate([r[...] for r in page_refs], axis=0)
    hi, mid, lo = _split3(x)
    sufm = sufm_ref[...]
    within = (_dot(hi, sufm) + _dot(mid, sufm)) + _dot(lo, sufm)
    ones = jnp.ones((LANES, LANES), BF16)
    tot = (_dot(hi, ones) + _dot(mid, ones)) + _dot(lo, ones)
    th, tmid, tl = _split3(tot)
    pgm = pgm_ref[...]
    later = (_dot(pgm, th) + _dot(pgm, tmid)) + _dot(pgm, tl)
    lfnew = lfnew_ref[...]
    bias = (within + later).reshape(npages, H_A, LANES) + lfnew[None]
    bias = bias.reshape(npages * H_A, LANES) * LOG2E

    bh = bias.astype(BF16).astype(F32)
    bl = (bias - bh).astype(BF16).astype(F32)
    half = LANES // 2
    lane = lax.broadcasted_iota(jnp.int32, bias.shape, 1)
    first = jnp.where(lane < half, bh, pltpu.roll(bl, half, 1))
    second = jnp.where(lane < half, pltpu.roll(bh, half, 1), bl)
    first = first.reshape(npages, 1, H_A, LANES)
    second = second.reshape(npages, 1, H_A, LANES)
    halves = jnp.concatenate([first, second], axis=1)
    o_ref[...] = jnp.concatenate([halves, halves], axis=2).astype(o_ref.dtype)


def _dec_bias(page_table, lfnew_rep, logf_t, layer):
    batch, npages = page_table.shape
    page = logf_t.shape[-1]
    assert page == LANES
    sufm = jnp.asarray(np.tril(np.ones((page, page), np.float32), -1), BF16)
    pg = np.arange(npages * H_A)
    pgm_np = ((pg[None, :] // H_A > pg[:, None] // H_A) & (pg[None, :] % H_A == pg[:, None] % H_A))
    pgm = jnp.asarray(pgm_np.astype(np.float32), BF16)

    def page_spec(j):
        return pl.BlockSpec((None, None, H_A, LANES), lambda b, pt: (layer, pt[b, j], 0, 0))

    kern = functools.partial(_dec_bias_kernel, npages=npages)
    grid_spec = pltpu.PrefetchScalarGridSpec(
        num_scalar_prefetch=1,
        grid=(batch,),
        in_specs=[pl.BlockSpec((None, H_A, LANES), lambda b, pt: (b, 0, 0)),
                  pl.BlockSpec((page, page), lambda b, pt: (0, 0)),
                  pl.BlockSpec(pgm.shape, lambda b, pt: (0, 0))]
                 + [page_spec(j) for j in range(npages)],
        out_specs=pl.BlockSpec((None, npages, 2, 2 * H_A, LANES), lambda b, pt: (b, 0, 0, 0, 0)),
    )
    return pl.pallas_call(
        kern,
        grid_spec=grid_spec,
        out_shape=jax.ShapeDtypeStruct((batch, npages, 2, 2 * H_A, LANES), BF16),
        compiler_params=_cparams(("arbitrary",), 32),
        name="dec_bias",
    )(page_table, lfnew_rep, sufm, pgm, *([logf_t] * npages))


def _dec_attn_kernel(pt_ref, q_ref, kn_ref, vn_ref, bias_ref, mask_ref, *rest, ppb, scale):
    k_refs = rest[:ppb]
    v_refs = rest[ppb:2 * ppb]
    o_ref = rest[2 * ppb]
    m_ref, l_ref, acc_ref = rest[2 * ppb + 1:]
    j = pl.program_id(1)

    @pl.when(j == 0)
    def _():
        m_ref[...] = jnp.full(m_ref.shape, NEG, F32)
        l_ref[...] = jnp.zeros_like(l_ref)
        acc_ref[...] = jnp.zeros_like(acc_ref)

    qs = q_ref[...] * (scale * LOG2E)
    ones = jnp.ones((2 * LANES, LANES), BF16)
    page = k_refs[0].shape[0]
    half = page // 2
    grp = 16
    ms, ls, accs = [], [], []
    for r in range(ppb):
        k = k_refs[r][...]
        prod = (k * qs[None]).reshape(page * H_A, DH_A).astype(BF16)
        lbs = []
        for hh in range(2):
            t2 = bias_ref[r, hh]
            lbs.append((t2[None] * mask_ref[...]).reshape(half * H_A, LANES))
        lhs = jnp.concatenate([prod, jnp.concatenate(lbs, axis=0)], axis=1)
        s = _dot(lhs, ones).reshape(page, H_A, LANES)
        for g in range(page // grp):
            sg = s[g * grp:(g + 1) * grp]
            m_g = jnp.max(sg, axis=0)
            p = jnp.exp2(sg - m_g[None])
            ms.append(m_g)
            ls.append(jnp.sum(p, axis=0))
            accs.append(jnp.sum(p * v_refs[r][g * grp:(g + 1) * grp], axis=0))
    m_old = m_ref[...]
    m = functools.reduce(jnp.maximum, ms, m_old)
    a_old = jnp.exp2(m_old - m)
    l = a_old * l_ref[...]
    acc = a_old * acc_ref[...]
    for m_g, l_g, acc_g in zip(ms, ls, accs):
        w = jnp.exp2(m_g - m)
        l = l + w * l_g
        acc = acc + w * acc_g
    m_ref[...], l_ref[...], acc_ref[...] = m, l, acc

    @pl.when(j == pl.num_programs(1) - 1)
    def _():
        s_new = jnp.sum(qs * kn_ref[...], axis=-1, keepdims=True)
        m_fin = jnp.maximum(m, s_new)
        p_new = jnp.exp2(s_new - m_fin)
        a = jnp.exp2(m - m_fin)
        o_ref[...] = (a * acc + p_new * vn_ref[...]) / (a * l + p_new)


def _dec_attn(page_table, q3, kn3, vn3, bias5, cache_k, cache_v, layer, ppb):
    batch, npages = page_table.shape
    page = cache_k.shape[2]
    half = page // 2
    mask_np = np.zeros((half // 2, 2 * H_A, LANES), np.float32)
    for i in range(half // 2):
        for s in range(2):
            pos = 2 * i + s
            mask_np[i, s * H_A:(s + 1) * H_A, pos] = 1.0
            mask_np[i, s * H_A:(s + 1) * H_A, half + pos] = 1.0
    mask = jnp.asarray(mask_np, BF16)
    vec = pl.BlockSpec((None, H_A, DH_A), lambda b, j, pt: (b, 0, 0))

    def page_spec(r):
        return pl.BlockSpec((None, None, page, H_A, DH_A),
                            lambda b, j, pt: (layer, pt[b, j * ppb + r], 0, 0, 0))

    kern = functools.partial(_dec_attn_kernel, ppb=ppb, scale=DH_A ** -0.5)
    grid_spec = pltpu.PrefetchScalarGridSpec(
        num_scalar_prefetch=1,
        grid=(batch, npages // ppb),
        in_specs=[vec, vec, vec,
                  pl.BlockSpec((None, ppb, 2, 2 * H_A, LANES), lambda b, j, pt: (b, j, 0, 0, 0)),
                  pl.BlockSpec(mask.shape, lambda b, j, pt: (0, 0, 0))]
                 + [page_spec(r) for r in range(ppb)] * 2,
        out_specs=vec,
        scratch_shapes=[pltpu.VMEM((H_A, DH_A), F32)] * 3,
    )
    return pl.pallas_call(
        kern,
        grid_spec=grid_spec,
        out_shape=jax.ShapeDtypeStruct((batch, H_A, DH_A), F32),
        compiler_params=_cparams(("arbitrary", "arbitrary"), 56),
        name="dec_attn",
    )(page_table, q3, kn3, vn3, bias5, mask, *([cache_k] * ppb), *([cache_v] * ppb))


def _dec_mem_kernel(q_ref, k_ref, v_ref, o_ref, *, scale):
    q = q_ref[...] * (scale * LOG2E)
    s = jnp.sum(k_ref[...] * q[None], axis=-1, keepdims=True)
    p = jnp.exp2(s - jnp.max(s, axis=0, keepdims=True))
    num = jnp.sum(p * v_ref[...], axis=0)
    o_ref[...] = num / jnp.sum(p, axis=0)


def _dec_mem_attn(qm, cache_km, cache_vm, layer):
    batch, hm, dhm = qm.shape
    nmem = cache_km.shape[2]
    kern = functools.partial(_dec_mem_kernel, scale=dhm ** -0.5)
    vec = pl.BlockSpec((None, hm, dhm), lambda b: (b, 0, 0))
    mem = pl.BlockSpec((None, None, nmem, hm, dhm), lambda b: (layer, b, 0, 0, 0))
    return pl.pallas_call(
        kern,
        grid=(batch,),
        in_specs=[vec, mem, mem],
        out_specs=vec,
        out_shape=jax.ShapeDtypeStruct((batch, hm, dhm), F32),
        compiler_params=_cparams(("arbitrary",), 32),
        name="dec_mem_attn",
    )(qm, cache_km, cache_vm)


def _dec_tail_kernel(z_ref, oa_ref, om_ref, state_ref, x_ref, wpool_ref, pscale_ref, wout_ref,
                     npost_ref, npre_ref, xo_ref, ho_ref, *, past_len):
    z = lambda g: z_ref[g]
    u = z(C_U)
    d = u.shape[1]
    gw = d // N_POOL
    dms = []
    for g, w in enumerate(POOL_WINDOWS):
        sl = slice(g * gw, (g + 1) * gw)
        tot = u[:, sl]
        for t in range(1, w):
            tot = tot + state_ref[POOL_STATE - t, :, sl]
        dms.append(tot * (1.0 / min(w, past_len + 1)) - u[:, sl])
    ob = jnp.concatenate([_dot(dms[g].astype(BF16), wpool_ref[g]) for g in range(N_POOL)], axis=1)
    ob = ob * pscale_ref[...]
    hmix = _merge(oa_ref[...], z(C_GA), ob, z(C_GB), om_ref[...], z(C_GM), z(C_ZA), z(C_ZB), z(C_ZM))
    y = _dot(hmix.astype(BF16), wout_ref[...])
    xn = x_ref[...] + _rms(y) * npost_ref[...]
    xo_ref[...] = xn
    ho_ref[...] = (_rms(xn) * npre_ref[...]).astype(ho_ref.dtype)


def _dec_tail(z, oa, om, state_t, x2, wpool, pscale, wout, npost, npre_next, past_len):
    b, d = x2.shape
    return pl.pallas_call(
        functools.partial(_dec_tail_kernel, past_len=past_len),
        out_shape=[jax.ShapeDtypeStruct((b, d), F32),
                   jax.ShapeDtypeStruct((b, d), BF16)],
        compiler_params=pltpu.CompilerParams(vmem_limit_bytes=48 * 1024 * 1024),
        name="dec_tail",
    )(z, oa, om, state_t, x2, wpool, pscale, wout, npost, npre_next)


def _split_in_proj(w_in):
    d = w_in.shape[1]
    f0 = 3 * d
    wg = jnp.concatenate([w_in[:, :, :f0], w_in[:, :, f0 + H_A:]], axis=-1).astype(BF16)
    wf = jnp.pad(w_in[:, :, f0:f0 + H_A], ((0, 0), (0, 0), (0, LANES - H_A))).astype(BF16)
    return wg, wf


def kernel(x_prompt, x_sample, mem_prompt, cache_k_attn, cache_v_attn, cache_logf_attn, cache_k_mem,
           cache_v_mem, state_pool, page_table, norm_pre, w_in, b_forget, w_pool, pool_scale, norm_mem,
           w_mem_kv, w_out, norm_post):
    bp, sp, d = x_prompt.shape
    bd, sd, _ = x_sample.shape
    assert sd == 1 and d == H_A * DH_A
    depth = w_in.shape[0]
    nmem = mem_prompt.shape[1]
    mp = bp * sp

    wg, wf = _split_in_proj(w_in)
    bfp = jnp.pad(b_forget, ((0, 0), (0, LANES - H_A)))[:, None, :]
    wpool = w_pool.astype(BF16)
    wout = w_out.astype(BF16)
    wmem = w_mem_kv.astype(BF16)
    npre = jnp.concatenate([norm_pre, norm_pre[:1]], axis=0)[:, None, :]
    npost = norm_post[:, None, :]
    pscale = pool_scale[:, None, :]
    logf_t = jnp.swapaxes(cache_logf_attn, 2, 3)
    state_t = jnp.swapaxes(state_pool, 1, 2)
    past_len = page_table.shape[1] * cache_k_attn.shape[2]

    tm_proj = min(1024, sp)
    tm_merge = min(256, sp)
    tq = min(1024, sp)
    tk = min(512, tq)
    ts = min(256, tk)
    ppb = next(n for n in (16, 8, 1) if page_table.shape[1] % n == 0)
    q_scale = DH_A ** -0.5 * LOG2E

    def bf16_cols(g):
        return jnp.where(g == 0, C_Q, jnp.where(g == 1, C_GA, g + (C_GB - 2)))

    xp = x_prompt
    xs = x_sample.reshape(bd, d)
    hp = _rmsnorm_bf16(xp.reshape(mp, d), norm_pre[0], min(512, mp))
    hs = _rmsnorm_bf16(xs, norm_pre[0], bd)
    mem2 = mem_prompt.reshape(bp * nmem, d)

    k_all = v_all = None
    outs = {k: [] for k in ("fp", "kmp", "vmp", "pp", "kd", "vd", "fd", "pd")}
    for l in range(depth):
        hp3 = hp.reshape(bp, sp, d)
        k_all, k16 = _project_cache(hp3, wg, C_K, k_all, l, depth, tm_proj)
        v_all, v16 = _project_cache(hp3, wg, C_V, v_all, l, depth, tm_proj)
        (u32,) = _project(hp, wg, l, lambda g: C_U, 1, (F32,), tm_proj)
        (zb8,) = _project(hp, wg, l, bf16_cols, 11 - N_F32_GROUPS, (BF16,), tm_proj, first_scale=q_scale)
        logf, qx, kx = _forget_gate(hp, wf[l], bfp[l], bp, sp, min(512, sp))
        u32 = u32.reshape(bp, sp, d)
        zb8 = zb8.reshape(11 - N_F32_GROUPS, bp, sp, d)
        oa = _fox_prompt(zb8, qx, k16, kx, v16, tq, tk, ts)

        hm = _rmsnorm_bf16(mem2, norm_mem[l], min(512, bp * nmem))
        kvm32, kvm16 = _project(hm, wmem, l, lambda g: g, 2, (F32, BF16), min(512, bp * nmem))
        xp, hp3 = _prompt_merge(oa, zb8, u32, xp, kvm16.reshape(2, bp, nmem, d), wpool[l], pscale[l],
                                wout[l], npost[l], npre[l + 1], tm_merge)
        hp = hp3.reshape(mp, d)
        outs["fp"].append(logf.reshape(bp, sp, LANES)[:, :, :H_A])
        outs["kmp"].append(kvm32[0].reshape(bp, nmem, H_M, d // H_M))
        outs["vmp"].append(kvm32[1].reshape(bp, nmem, H_M, d // H_M))
        outs["pp"].append(u32[:, sp - POOL_STATE:])

        (zs,) = _project(hs, wg, l, lambda g: g, 11, (F32,), bd)
        lfnew = _dec_forget_gate(hs, wf[l], bfp[l])
        lfnew_rep = jnp.broadcast_to(lfnew[:, :H_A, None], (bd, H_A, LANES))
        bias5 = _dec_bias(page_table, lfnew_rep, logf_t, l)
        q3 = zs[C_Q].reshape(bd, H_A, DH_A)
        kn3 = zs[C_K].reshape(bd, H_A, DH_A)
        vn3 = zs[C_V].reshape(bd, H_A, DH_A)
        oa_s = _dec_attn(page_table, q3, kn3, vn3, bias5, cache_k_attn, cache_v_attn, l, ppb)
        om_s = _dec_mem_attn(zs[C_QM].reshape(bd, H_M, d // H_M), cache_k_mem, cache_v_mem, l)
        xs, hs = _dec_tail(zs, oa_s.reshape(bd, d), om_s.reshape(bd, d), state_t[l], xs,
                           wpool[l], pscale[l], wout[l], npost[l], npre[l + 1], past_len)
        st_new = jnp.concatenate([state_pool[l][:, 1:], zs[C_U][:, None, :]], axis=1)
        outs["kd"].append(kn3.reshape(bd, 1, H_A, DH_A))
        outs["vd"].append(vn3.reshape(bd, 1, H_A, DH_A))
        outs["fd"].append(lfnew[:, None, :H_A])
        outs["pd"].append(st_new)

    st = {k: jnp.stack(v) for k, v in outs.items()}
    return (xp, xs.reshape(bd, 1, d), k_all, v_all, st["fp"], st["kmp"], st["vmp"], st["pp"],
            st["kd"], st["vd"], st["fd"], st["pd"])
```
